```python
import jax, jax.numpy as jnp
from jax import lax
import numpy as np

D_MODEL = 1024
BATCH = 8
SEQ = 4096
DEPTH = 1

MLA_HEADS = 8
MLA_NOPE_DIM = 64
MLA_ROPE_DIM = 32
MLA_V_DIM = 64
MLA_Q_RANK = 384
MLA_KV_RANK = 256
MLA_QK_DIM = MLA_NOPE_DIM + MLA_ROPE_DIM
RET_HEADS = 4
RET_QK_DIM = 64
RET_V_DIM = 128
CHUNK = 128
Q_BLOCK = 128

MLA_WIDTH = MLA_HEADS * MLA_V_DIM
RET_WIDTH = RET_HEADS * RET_V_DIM
D_MIX = MLA_WIDTH + RET_WIDTH
IN_SPLITS = (MLA_Q_RANK, MLA_KV_RANK, MLA_ROPE_DIM,
             RET_HEADS * RET_QK_DIM, RET_HEADS * RET_QK_DIM, RET_WIDTH, RET_WIDTH)
IN_COLS = sum(IN_SPLITS)
D_FF = ((8 * D_MODEL + 3 * 256 - 1) // (3 * 256)) * 256
ROPE_BASE = 10000.0
EPS = 1e-6

kernel_name = "hybrid_mla_retention_sandwich_adaln"


def rmsnorm(x, g):
    x32 = x.astype(jnp.float32)
    r = x32 * lax.rsqrt(jnp.mean(x32 * x32, axis=-1, keepdims=True) + EPS)
    return (r * g.astype(jnp.float32)).astype(x.dtype)


def rope(x, pos):
    d = x.shape[-1]
    inv = ROPE_BASE ** (-jnp.arange(0, d, 2, dtype=jnp.float32) / d)
    ang = pos.astype(jnp.float32)[:, :, None, None] * inv
    cos, sin = jnp.cos(ang), jnp.sin(ang)
    x1, x2 = jnp.split(x.astype(jnp.float32), 2, axis=-1)
    return jnp.concatenate([x1 * cos - x2 * sin, x1 * sin + x2 * cos], axis=-1).astype(x.dtype)


def mla_group(cq_raw, ckv_raw, kpe_raw, pos, q_a_norm, w_q_b, kv_a_norm, w_kv_b):
    B, S, _ = cq_raw.shape
    cq = rmsnorm(cq_raw, q_a_norm)
    q = (cq @ w_q_b).reshape(B, S, MLA_HEADS, MLA_QK_DIM)
    q_nope, q_pe = q[..., :MLA_NOPE_DIM], rope(q[..., MLA_NOPE_DIM:], pos)
    ckv = rmsnorm(ckv_raw, kv_a_norm)
    kv = (ckv @ w_kv_b).reshape(B, S, MLA_HEADS, MLA_NOPE_DIM + MLA_V_DIM)
    k_nope, v = kv[..., :MLA_NOPE_DIM], kv[..., MLA_NOPE_DIM:]
    k_pe = rope(kpe_raw[:, :, None, :], pos)
    q = jnp.concatenate([q_nope, q_pe], axis=-1)
    k = jnp.concatenate([k_nope, jnp.broadcast_to(k_pe, (B, S, MLA_HEADS, MLA_ROPE_DIM))], axis=-1)
    scale = MLA_QK_DIM ** -0.5
    nb = S // Q_BLOCK
    qb = q.reshape(B, nb, Q_BLOCK, MLA_HEADS, MLA_QK_DIM).transpose(1, 0, 3, 2, 4)
    kt = k.transpose(0, 2, 1, 3)
    vt = v.transpose(0, 2, 1, 3)
    k_idx = jnp.arange(S)

    def block(args):
        q_blk, start = args
        s = jnp.einsum('bhqd,bhkd->bhqk', q_blk, kt, preferred_element_type=jnp.float32) * scale
        q_idx = start + jnp.arange(Q_BLOCK)
        s = jnp.where(k_idx[None, :] <= q_idx[:, None], s, -jnp.inf)
        p = jax.nn.softmax(s, axis=-1).astype(vt.dtype)
        return jnp.einsum('bhqk,bhkd->bhqd', p, vt)

    o = lax.map(block, (qb, jnp.arange(nb) * Q_BLOCK))
    return o.transpose(1, 0, 3, 2, 4).reshape(B, S, MLA_WIDTH)


def retention_group(q_raw, k_raw, v_raw, g_raw, pos, gn_gain):
    B, S, _ = q_raw.shape
    f32 = jnp.float32
    q = rope(q_raw.reshape(B, S, RET_HEADS, RET_QK_DIM), pos).astype(f32)
    k = rope(k_raw.reshape(B, S, RET_HEADS, RET_QK_DIM), pos).astype(f32) * (RET_QK_DIM ** -0.5)
    v = v_raw.reshape(B, S, RET_HEADS, RET_V_DIM).astype(f32)
    log_gamma = jnp.log(1.0 - 2.0 ** (-5.0 - jnp.arange(RET_HEADS, dtype=f32)))
    nc = S // CHUNK
    to_chunks = lambda t: t.reshape(B, nc, CHUNK, RET_HEADS, t.shape[-1]).transpose(0, 3, 1, 2, 4)
    qc, kc, vc = to_chunks(q), to_chunks(k), to_chunks(v)
    idx = jnp.arange(CHUNK)
    rel = idx[:, None] - idx[None, :]
    decay_in = jnp.where(rel >= 0, jnp.exp(log_gamma[:, None, None] * jnp.maximum(rel, 0).astype(f32)), 0.0)
    scores = jnp.einsum('bhncd,bhnmd->bhncm', qc, kc) * decay_in[None, :, None]
    inner = jnp.einsum('bhncm,bhnmv->bhncv', scores, vc)
    w_k = jnp.exp(log_gamma[:, None] * (CHUNK - 1 - idx).astype(f32))
    u = jnp.einsum('bhncd,bhnce->bhnde', kc * w_k[None, :, None, :, None], vc)
    chunk_decay = jnp.exp(log_gamma * CHUNK)[None, :, None, None]

    def step(state, u_i):
        return state * chunk_decay + u_i, state

    _, s_prev = lax.scan(step, jnp.zeros((B, RET_HEADS, RET_QK_DIM, RET_V_DIM), f32),
                         u.transpose(2, 0, 1, 3, 4))
    s_prev = s_prev.transpose(1, 2, 0, 3, 4)
    w_q = jnp.exp(log_gamma[:, None] * (idx + 1).astype(f32))
    cross = jnp.einsum('bhncd,bhnde->bhnce', qc * w_q[None, :, None, :, None], s_prev)
    o = (inner + cross).transpose(0, 2, 3, 1, 4).reshape(B, S, RET_HEADS, RET_V_DIM)
    mu = jnp.mean(o, axis=-1, keepdims=True)
    var = jnp.mean(jnp.square(o - mu), axis=-1, keepdims=True)
    o = ((o - mu) * lax.rsqrt(var + EPS)).reshape(B, S, RET_WIDTH) * gn_gain.astype(f32)
    return (jax.nn.silu(g_raw.astype(f32)) * o).astype(q_raw.dtype)


def setup_inputs(seed: int = 0) -> dict:
    key = jax.random.key(seed)
    ks = jax.random.split(key, 24)
    L = DEPTH
    nrm = lambda k, shape, fan_in: jax.random.normal(k, shape, jnp.float32) * fan_in ** -0.5
    gain = lambda k, n: 1.0 + 0.05 * jax.random.normal(k, (L, n), jnp.float32)
    return {
        "x": jax.random.normal(ks[0], (BATCH, SEQ, D_MODEL), jnp.float32),
        "c": jax.random.normal(ks[1], (BATCH, D_MODEL), jnp.float32),
        "positions": (jax.random.randint(ks[2], (BATCH, 1), 0, 512, jnp.int32)
                      + jnp.arange(SEQ, dtype=jnp.int32)[None, :]),
        "w_ada": 0.5 * nrm(ks[3], (L, D_MODEL, 6 * D_MODEL), D_MODEL),
        "b_ada": 0.01 * jax.random.normal(ks[4], (L, 6 * D_MODEL), jnp.float32),
        "pre_norm_mix": gain(ks[5], D_MODEL),
        "w_in": nrm(ks[6], (L, D_MODEL, IN_COLS), D_MODEL),
        "q_a_norm": gain(ks[7], MLA_Q_RANK),
        "w_q_b": nrm(ks[8], (L, MLA_Q_RANK, MLA_HEADS * MLA_QK_DIM), MLA_Q_RANK),
        "kv_a_norm": gain(ks[9], MLA_KV_RANK),
        "w_kv_b": nrm(ks[10], (L, MLA_KV_RANK, MLA_HEADS * (MLA_NOPE_DIM + MLA_V_DIM)), MLA_KV_RANK),
        "mla_out_norm": gain(ks[11], MLA_WIDTH),
        "ret_gn_gain": gain(ks[12], RET_WIDTH),
        "w_out": nrm(ks[13], (L, D_MIX, D_MODEL), D_MIX),
        "post_norm_mix": gain(ks[14], D_MODEL),
        "pre_norm_ffn": gain(ks[15], D_MODEL),
        "w_gate": nrm(ks[16], (L, D_MODEL, D_FF), D_MODEL),
        "w_up": nrm(ks[17], (L, D_MODEL, D_FF), D_MODEL),
        "w_down": nrm(ks[18], (L, D_FF, D_MODEL), D_FF),
        "post_norm_ffn": gain(ks[19], D_MODEL),
    }


def reference(x, c, positions, w_ada, b_ada, pre_norm_mix, w_in, q_a_norm, w_q_b, kv_a_norm,
              w_kv_b, mla_out_norm, ret_gn_gain, w_out, post_norm_mix, pre_norm_ffn,
              w_gate, w_up, w_down, post_norm_ffn):
    offsets = np.cumsum(IN_SPLITS)[:-1].tolist()
    for l in range(DEPTH):
        mod = (jax.nn.silu(c) @ w_ada[l] + b_ada[l])[:, None, :]
        sh1, sc1, g1, sh2, sc2, g2 = jnp.split(mod, 6, axis=-1)
        h = rmsnorm(x, pre_norm_mix[l]) * (1.0 + sc1) + sh1
        z = h @ w_in[l]
        cq, ckv, kpe, rq, rk, rv, rg = jnp.split(z, offsets, axis=-1)
        y_mla = rmsnorm(mla_group(cq, ckv, kpe, positions, q_a_norm[l], w_q_b[l],
                                  kv_a_norm[l], w_kv_b[l]), mla_out_norm[l])
        y_ret = retention_group(rq, rk, rv, rg, positions, ret_gn_gain[l])
        mix = jnp.concatenate([y_mla, y_ret], axis=-1) @ w_out[l]
        x = x + g1 * rmsnorm(mix, post_norm_mix[l])
        h = rmsnorm(x, pre_norm_ffn[l]) * (1.0 + sc2) + sh2
        f = (jax.nn.silu(h @ w_gate[l]) * (h @ w_up[l])) @ w_down[l]
        x = x + g2 * rmsnorm(f, post_norm_ffn[l])
    return x
```

```python
import functools

import jax
import jax.numpy as jnp
import numpy as np
from jax import lax
from jax.experimental import pallas as pl
from jax.experimental.pallas import tpu as pltpu

MLA_HEADS = 8
MLA_NOPE = 64
MLA_ROPE = 32
MLA_V = 64
MLA_Q_RANK = 384
MLA_KV_RANK = 256
MLA_QK = MLA_NOPE + MLA_ROPE
RET_HEADS = 4
RET_QK = 64
RET_V = 128
MLA_WIDTH = MLA_HEADS * MLA_V
RET_WIDTH = RET_HEADS * RET_V
ROPE_BASE = 10000.0
EPS = 1e-6

LANES = 128
VMEM_LIMIT = 56 * 1024 * 1024

TM_IN = 512
TQ = 512
RET_C = 256
RET_T = 512
TM_OUT = 512
FF_CHUNK = 256

HEAD_LANES = LANES
ROPE_HALF = MLA_ROPE // 2
RET_HALF = RET_QK // 2
N_FREQ = ROPE_HALF + RET_HALF


def _bf16(x):
    return x.astype(jnp.bfloat16)


def _dot(a, b):
    return jnp.dot(a, b, preferred_element_type=jnp.float32)


def _dot_nt(a, b):
    return lax.dot_general(a, b, (((1,), (1,)), ((), ())), preferred_element_type=jnp.float32)


def _dot_tn(a, b):
    return lax.dot_general(a, b, (((0,), (0,)), ((), ())), preferred_element_type=jnp.float32)


def _rms(x, gain):
    return x * lax.rsqrt(jnp.mean(x * x, axis=-1, keepdims=True) + EPS) * gain


def _const_spec(shape):
    nd = len(shape)
    return pl.BlockSpec(shape, lambda *_: (0,) * nd, pipeline_mode=pl.Buffered(1))


def _mod_kernel(c_ref, w_ref, b_ref, o_ref):
    c = c_ref[...]
    a = _bf16(c * jax.nn.sigmoid(c))
    o_ref[...] = _dot(a, _bf16(w_ref[...])) + b_ref[...]


def _mod_call(c, w_ada, b_ada):
    bsz, d = c.shape
    n = w_ada.shape[1]
    tn = 1024
    return pl.pallas_call(
        _mod_kernel,
        grid=(n // tn,),
        in_specs=[pl.BlockSpec((bsz, d), lambda j: (0, 0)),
                  pl.BlockSpec((d, tn), lambda j: (0, j)),
                  pl.BlockSpec((1, tn), lambda j: (0, j))],
        out_specs=pl.BlockSpec((bsz, tn), lambda j: (0, j)),
        out_shape=jax.ShapeDtypeStruct((bsz, n), jnp.float32),
        compiler_params=pltpu.CompilerParams(dimension_semantics=("arbitrary",),
                                             vmem_limit_bytes=VMEM_LIMIT),
        name="adaln_mod",
    )(c, w_ada, b_ada.reshape(1, n))


_O_CQ = 0
_O_CKV = _O_CQ + MLA_Q_RANK
_O_KPE = _O_CKV + MLA_KV_RANK
_O_RQ = _O_KPE + HEAD_LANES
_O_RK = _O_RQ + RET_HEADS * RET_QK
_O_RV = _O_RK + RET_HEADS * RET_QK
_O_RG = _O_RV + RET_WIDTH
_IN_COLS_P = _O_RG + RET_WIDTH


def _swap_rope_halves(x):
    lane = lax.broadcasted_iota(jnp.int32, x.shape, 1)
    from_right = pltpu.roll(x, HEAD_LANES - ROPE_HALF, 1)
    from_left = pltpu.roll(x, ROPE_HALF, 1)
    return jnp.where(lane < MLA_NOPE + ROPE_HALF, from_right, from_left)


def _inproj_kernel(x_ref, sc_ref, sh_ref, pos_ref, inv_ref, gpre_ref, win_ref, gq_ref, wqb_ref,
                   gkv_ref, wk_ref, wv_ref,
                   q_ref, k_ref, v_ref, rq_ref, rk_ref, rv_ref, rg_ref):
    tm = x_ref.shape[0]
    x = x_ref[...]
    h = _bf16(_rms(x, gpre_ref[...]) * (1.0 + sc_ref[0]) + sh_ref[0])

    pos = pos_ref[0].astype(jnp.float32)
    ang = inv_ref[...] * pos
    cs = jnp.cos(ang)
    sn = jnp.sin(ang)
    c_m, c_r = cs[:ROPE_HALF], cs[ROPE_HALF:]
    s_m, s_r = sn[:ROPE_HALF], sn[ROPE_HALF:]
    ones = jnp.ones((MLA_NOPE, tm), jnp.float32)
    zeros_hi = jnp.zeros((HEAD_LANES - MLA_QK, tm), jnp.float32)
    zeros_lo = jnp.zeros((MLA_NOPE, tm), jnp.float32)
    cos_k = jnp.concatenate([ones, c_m, c_m, zeros_hi], axis=0).T
    sin_k = jnp.concatenate([zeros_lo, -s_m, s_m, zeros_hi], axis=0).T
    cos_r = jnp.concatenate([c_r] * RET_HEADS, axis=0).T
    sin_r = jnp.concatenate([s_r] * RET_HEADS, axis=0).T
    scale = MLA_QK ** -0.5
    cos_q = cos_k * scale
    sin_q = sin_k * scale

    cq = _dot(h, win_ref[:, _O_CQ:_O_CQ + MLA_Q_RANK])
    q = _dot(_bf16(_rms(cq, gq_ref[...])), wqb_ref[...])
    for hd in range(MLA_HEADS):
        sl = slice(hd * HEAD_LANES, (hd + 1) * HEAD_LANES)
        qh = q[:, sl]
        q_ref[:, sl] = _bf16(qh * cos_q + _swap_rope_halves(qh) * sin_q)

    ckv = _dot(h, win_ref[:, _O_CKV:_O_CKV + MLA_KV_RANK])
    ckvn = _bf16(_rms(ckv, gkv_ref[...]))
    kpe = _dot(h, win_ref[:, _O_KPE:_O_KPE + HEAD_LANES])
    kpe = kpe * cos_k + _swap_rope_halves(kpe) * sin_k
    kn = _dot(ckvn, wk_ref[...])
    for hd in range(MLA_HEADS):
        sl = slice(hd * HEAD_LANES, (hd + 1) * HEAD_LANES)
        k_ref[:, sl] = _bf16(kn[:, sl] + kpe)
    v_ref[...] = _bf16(_dot(ckvn, wv_ref[...]))

    for off, o_ref in ((_O_RQ, rq_ref), (_O_RK, rk_ref)):
        z = _dot(h, win_ref[:, off:off + 2 * LANES])
        z1, z2 = z[:, :LANES], z[:, LANES:]
        o_ref[:, :LANES] = _bf16(z1 * cos_r - z2 * sin_r)
        o_ref[:, LANES:] = _bf16(z1 * sin_r + z2 * cos_r)
    rv_ref[...] = _bf16(_dot(h, win_ref[:, _O_RV:_O_RV + RET_WIDTH]))
    rg_ref[...] = _bf16(_dot(h, win_ref[:, _O_RG:_O_RG + RET_WIDTH]))


def _inproj_call(x2, mod3, pos3, inv_col, gpre, win_p, gq, wqb_p, gkv, wk_p, wv_p, seq):
    t, d = x2.shape
    tm = TM_IN
    per_b = seq // tm
    tok = lambda n: pl.BlockSpec((tm, n), lambda i: (i, 0))
    mod_spec = lambda k: pl.BlockSpec((1, 1, d), lambda i: ((i // per_b) * 6 + k, 0, 0))
    outs = [(MLA_HEADS * HEAD_LANES), (MLA_HEADS * HEAD_LANES), MLA_WIDTH,
            RET_HEADS * RET_QK, RET_HEADS * RET_QK, RET_WIDTH, RET_WIDTH]
    return pl.pallas_call(
        _inproj_kernel,
        grid=(t // tm,),
        in_specs=[tok(d), mod_spec(1), mod_spec(0),
                  pl.BlockSpec((1, 1, tm), lambda i: (i, 0, 0)),
                  _const_spec(inv_col.shape), _const_spec(gpre.shape), _const_spec(win_p.shape),
                  _const_spec(gq.shape), _const_spec(wqb_p.shape), _const_spec(gkv.shape),
                  _const_spec(wk_p.shape), _const_spec(wv_p.shape)],
        out_specs=[tok(n) for n in outs],
        out_shape=[jax.ShapeDtypeStruct((t, n), jnp.bfloat16) for n in outs],
        compiler_params=pltpu.CompilerParams(dimension_semantics=("arbitrary",),
                                             vmem_limit_bytes=VMEM_LIMIT),
        name="inproj",
    )(x2, mod3, mod3, pos3, inv_col, gpre, win_p, gq, wqb_p, gkv, wk_p, wv_p)


def _attn_kernel(q_ref, k_ref, v_ref, o_ref, m_ref, l_ref, acc_ref):
    qi = pl.program_id(2)
    tq = q_ref.shape[1]
    m_ref[...] = jnp.full(m_ref.shape, -jnp.inf, jnp.float32)
    l_ref[...] = jnp.zeros(l_ref.shape, jnp.float32)
    acc_ref[...] = jnp.zeros(acc_ref.shape, jnp.float32)

    def step(j, masked):
        start = pl.multiple_of(j * tq, tq)
        v = v_ref[0, pl.ds(start, tq), :]
        for hd in range(2):
            sl = slice(hd * HEAD_LANES, (hd + 1) * HEAD_LANES)
            s = _dot_nt(q_ref[0, :, sl], k_ref[0, pl.ds(start, tq), sl])
            if masked:
                row = lax.broadcasted_iota(jnp.int32, s.shape, 0)
                col = lax.broadcasted_iota(jnp.int32, s.shape, 1)
                s = jnp.where(col <= row, s, -jnp.inf)
            m_prev = m_ref[hd]
            m_new = jnp.maximum(m_prev, jnp.max(s, axis=-1, keepdims=True))
            alpha = jnp.exp(m_prev - m_new)
            p = jnp.exp(s - m_new)
            l_ref[hd] = alpha * l_ref[hd] + jnp.sum(p, axis=-1, keepdims=True)
            acc_ref[hd] = alpha * acc_ref[hd] + _dot(_bf16(p), v)
            m_ref[hd] = m_new

    def body(j, carry):
        step(j, False)
        return carry

    lax.fori_loop(0, qi, body, 0)
    step(qi, True)

    lane = lax.broadcasted_iota(jnp.int32, (tq, LANES), 1)
    o0 = acc_ref[0] / l_ref[0]
    o1 = acc_ref[1] / l_ref[1]
    o_ref[0] = _bf16(jnp.where(lane < MLA_V, o0, o1))


def _attn_call(q, k, v):
    bsz, seq, _ = q.shape
    tq = TQ
    pairs = MLA_HEADS // 2
    return pl.pallas_call(
        _attn_kernel,
        grid=(bsz, pairs, seq // tq),
        in_specs=[pl.BlockSpec((1, tq, 2 * HEAD_LANES), lambda b, p, i: (b, i, p)),
                  pl.BlockSpec((1, seq, 2 * HEAD_LANES), lambda b, p, i: (b, 0, p)),
                  pl.BlockSpec((1, seq, LANES), lambda b, p, i: (b, 0, p))],
        out_specs=pl.BlockSpec((1, tq, LANES), lambda b, p, i: (b, i, p)),
        out_shape=jax.ShapeDtypeStruct((bsz, seq, MLA_WIDTH), jnp.bfloat16),
        scratch_shapes=[pltpu.VMEM((2, tq, 1), jnp.float32),
                        pltpu.VMEM((2, tq, 1), jnp.float32),
                        pltpu.VMEM((2, tq, LANES), jnp.float32)],
        compiler_params=pltpu.CompilerParams(
            dimension_semantics=("arbitrary", "arbitrary", "arbitrary"),
            vmem_limit_bytes=VMEM_LIMIT),
        name="mla_attn",
    )(q, k, v)


def _ret_kernel(q_ref, k_ref, v_ref, g_ref, dec_ref, wq_ref, wk_ref, cd_ref, gain_ref,
                o_ref, state_ref):
    @pl.when(pl.program_id(1) == 0)
    def _():
        state_ref[...] = jnp.zeros(state_ref.shape, jnp.float32)

    c = RET_C
    lane = lax.broadcasted_iota(jnp.int32, (1, 2 * LANES), 1)
    for ci in range(q_ref.shape[1] // c):
        rows = slice(ci * c, (ci + 1) * c)
        q = q_ref[0, rows, :].astype(jnp.float32)
        k = k_ref[0, rows, :]
        kf = k.astype(jnp.float32)
        for hd in range(RET_HEADS):
            vh = v_ref[0, rows, hd * RET_V:(hd + 1) * RET_V]
            head = ((lane % LANES) // RET_HALF) == hd
            qm = jnp.where(head, q, 0.0)
            sc = _dot_nt(_bf16(qm), k) * dec_ref[hd]
            inner = _dot(_bf16(sc), vh)
            st = state_ref[hd]
            cross = _dot(_bf16(qm * wq_ref[hd]), _bf16(st))
            u = _dot_tn(_bf16(kf * wk_ref[hd]), vh)
            state_ref[hd] = st * cd_ref[hd] + u
            o = inner + cross
            mu = jnp.mean(o, axis=-1, keepdims=True)
            var = jnp.mean(jnp.square(o - mu), axis=-1, keepdims=True)
            cols = slice(hd * RET_V, (hd + 1) * RET_V)
            on = (o - mu) * lax.rsqrt(var + EPS) * gain_ref[:, cols]
            g = g_ref[0, rows, cols].astype(jnp.float32)
            o_ref[0, rows, cols] = _bf16(g * jax.nn.sigmoid(g) * on)


def _ret_tables():
    c = RET_C
    f32 = jnp.float32
    log_gamma = jnp.log(1.0 - 2.0 ** (-5.0 - jnp.arange(RET_HEADS, dtype=f32)))
    idx = jnp.arange(c)
    rel = idx[:, None] - idx[None, :]
    k_scale = RET_QK ** -0.5
    dec = jnp.where(rel >= 0, jnp.exp(log_gamma[:, None, None] * jnp.maximum(rel, 0).astype(f32)), 0.0)
    dec = dec * k_scale
    w_q = jnp.exp(log_gamma[:, None] * (idx + 1).astype(f32))[:, :, None]
    w_k = (jnp.exp(log_gamma[:, None] * (c - 1 - idx).astype(f32)) * k_scale)[:, :, None]
    cd = jnp.exp(log_gamma * c)[:, None, None]
    w_q = jnp.broadcast_to(w_q, (RET_HEADS, c, 2 * LANES))
    w_k = jnp.broadcast_to(w_k, (RET_HEADS, c, 2 * LANES))
    cd = jnp.broadcast_to(cd, (RET_HEADS, 1, LANES))
    return dec, w_q, w_k, cd


def _ret_call(rq, rk, rv, rg, gn_gain):
    bsz, seq, _ = rq.shape
    tt = RET_T
    dec, w_q, w_k, cd = _ret_tables()
    tok = lambda n: pl.BlockSpec((1, tt, n), lambda b, i: (b, i, 0))
    return pl.pallas_call(
        _ret_kernel,
        grid=(bsz, seq // tt),
        in_specs=[tok(2 * LANES), tok(2 * LANES), tok(RET_WIDTH), tok(RET_WIDTH),
                  _const_spec(dec.shape), _const_spec(w_q.shape), _const_spec(w_k.shape),
                  _const_spec(cd.shape), _const_spec(gn_gain.shape)],
        out_specs=tok(RET_WIDTH),
        out_shape=jax.ShapeDtypeStruct((bsz, seq, RET_WIDTH), jnp.bfloat16),
        scratch_shapes=[pltpu.VMEM((RET_HEADS, 2 * LANES, RET_V), jnp.float32)],
        compiler_params=pltpu.CompilerParams(dimension_semantics=("arbitrary", "arbitrary"),
                                             vmem_limit_bytes=VMEM_LIMIT),
        name="retention",
    )(rq, rk, rv, rg, dec, w_q, w_k, cd, gn_gain)


def _outffn_kernel(x_ref, ym_ref, yr_ref, g1_ref, sh2_ref, sc2_ref, g2_ref,
                   gmla_ref, gpost_ref, gpre2_ref, gpost2_ref,
                   wout_ref, wg_ref, wu_ref, wd_ref, o_ref, h_ref, a_ref):
    ymn = _bf16(_rms(ym_ref[...].astype(jnp.float32), gmla_ref[...]))
    mix = _dot(ymn, wout_ref[:MLA_WIDTH, :]) + _dot(yr_ref[...], wout_ref[MLA_WIDTH:, :])
    x1 = x_ref[...] + g1_ref[0] * _rms(mix, gpost_ref[...])
    o_ref[...] = x1
    h_ref[...] = _bf16(_rms(x1, gpre2_ref[...]) * (1.0 + sc2_ref[0]) + sh2_ref[0])

    n_chunks = wg_ref.shape[0]
    for ci in range(n_chunks):
        h = h_ref[...]
        g = _dot(h, wg_ref[ci])
        u = _dot(h, wu_ref[ci])
        a_ref[:, ci * FF_CHUNK:(ci + 1) * FF_CHUNK] = _bf16(g * jax.nn.sigmoid(g) * u)
    f = _dot(a_ref[...], wd_ref[...])
    o_ref[...] = o_ref[...] + g2_ref[0] * _rms(f, gpost2_ref[...])


def _outffn_call(x2, ym, yr, mod3, gmla, gpost, gpre2, gpost2, wout, wg3, wu3, wd, seq):
    t, d = x2.shape
    tm = TM_OUT
    per_b = seq // tm
    d_ff = wd.shape[0]
    tok = lambda n: pl.BlockSpec((tm, n), lambda i: (i, 0))
    mod_spec = lambda k: pl.BlockSpec((1, 1, d), lambda i: ((i // per_b) * 6 + k, 0, 0))
    return pl.pallas_call(
        _outffn_kernel,
        grid=(t // tm,),
        in_specs=[tok(d), tok(MLA_WIDTH), tok(RET_WIDTH),
                  mod_spec(2), mod_spec(3), mod_spec(4), mod_spec(5),
                  _const_spec(gmla.shape), _const_spec(gpost.shape), _const_spec(gpre2.shape),
                  _const_spec(gpost2.shape), _const_spec(wout.shape), _const_spec(wg3.shape),
                  _const_spec(wu3.shape), _const_spec(wd.shape)],
        out_specs=tok(d),
        out_shape=jax.ShapeDtypeStruct((t, d), jnp.float32),
        scratch_shapes=[pltpu.VMEM((tm, d), jnp.bfloat16), pltpu.VMEM((tm, d_ff), jnp.bfloat16)],
        compiler_params=pltpu.CompilerParams(dimension_semantics=("arbitrary",),
                                             vmem_limit_bytes=VMEM_LIMIT),
        name="outproj_ffn",
    )(x2, ym, yr, mod3, mod3, mod3, mod3, gmla, gpost, gpre2, gpost2, wout, wg3, wu3, wd)


def _relayout_w_in(w_in):
    d = w_in.shape[0]
    o = np.cumsum([0, MLA_Q_RANK, MLA_KV_RANK, MLA_ROPE, RET_HEADS * RET_QK, RET_HEADS * RET_QK,
                   RET_WIDTH, RET_WIDTH])
    cq, ckv, kpe, rq, rk, rv, rg = [w_in[:, o[i]:o[i + 1]] for i in range(7)]
    kpe = jnp.concatenate([jnp.zeros((d, MLA_NOPE), w_in.dtype), kpe,
                           jnp.zeros((d, HEAD_LANES - MLA_QK), w_in.dtype)], axis=1)

    def halves_first(w):
        return w.reshape(d, RET_HEADS, 2, RET_HALF).transpose(0, 2, 1, 3).reshape(d, RET_HEADS * RET_QK)

    return _bf16(jnp.concatenate([cq, ckv, kpe, halves_first(rq), halves_first(rk), rv, rg], axis=1))


def _relayout_w_q_b(w):
    r = w.shape[0]
    w = w.reshape(r, MLA_HEADS, MLA_QK)
    w = jnp.pad(w, ((0, 0), (0, 0), (0, HEAD_LANES - MLA_QK)))
    return _bf16(w.reshape(r, MLA_HEADS * HEAD_LANES))


def _relayout_w_kv_b(w):
    r = w.shape[0]
    w = w.reshape(r, MLA_HEADS, MLA_NOPE + MLA_V)
    wk = jnp.pad(w[..., :MLA_NOPE], ((0, 0), (0, 0), (0, HEAD_LANES - MLA_NOPE)))
    wv = w[..., MLA_NOPE:]
    return _bf16(wk.reshape(r, MLA_HEADS * HEAD_LANES)), _bf16(wv.reshape(r, MLA_WIDTH))


def _rope_inv_col():
    inv_m = ROPE_BASE ** (-jnp.arange(0, MLA_ROPE, 2, dtype=jnp.float32) / MLA_ROPE)
    inv_r = ROPE_BASE ** (-jnp.arange(0, RET_QK, 2, dtype=jnp.float32) / RET_QK)
    return jnp.concatenate([inv_m, inv_r]).reshape(N_FREQ, 1)


def kernel(x, c, positions, w_ada, b_ada, pre_norm_mix, w_in, q_a_norm, w_q_b, kv_a_norm, w_kv_b,
           mla_out_norm, ret_gn_gain, w_out, post_norm_mix, pre_norm_ffn, w_gate, w_up, w_down,
           post_norm_ffn):
    bsz, seq, d = x.shape
    t = bsz * seq
    depth = w_ada.shape[0]
    d_ff = w_gate.shape[-1]
    assert seq % TM_IN == 0 and seq % TQ == 0 and seq % RET_T == 0 and seq % TM_OUT == 0
    assert RET_T % RET_C == 0 and d_ff % FF_CHUNK == 0

    row = lambda g: g.reshape(1, -1)
    pos3 = positions.reshape(t // TM_IN, 1, TM_IN)
    inv_col = _rope_inv_col()
    x2 = x.reshape(t, d)
    for l in range(depth):
        mod3 = _mod_call(c, w_ada[l], b_ada[l]).reshape(bsz * 6, 1, d)
        wk_p, wv_p = _relayout_w_kv_b(w_kv_b[l])
        q, k, v, rq, rk, rv, rg = _inproj_call(
            x2, mod3, pos3, inv_col, row(pre_norm_mix[l]), _relayout_w_in(w_in[l]),
            row(q_a_norm[l]), _relayout_w_q_b(w_q_b[l]), row(kv_a_norm[l]), wk_p, wv_p, seq)
        b3 = lambda a: a.reshape(bsz, seq, a.shape[-1])
        y_mla = _attn_call(b3(q), b3(k), b3(v))
        y_ret = _ret_call(b3(rq), b3(rk), b3(rv), b3(rg), row(ret_gn_gain[l]))
        n_ch = d_ff // FF_CHUNK
        wg3 = _bf16(w_gate[l]).reshape(d, n_ch, FF_CHUNK).transpose(1, 0, 2)
        wu3 = _bf16(w_up[l]).reshape(d, n_ch, FF_CHUNK).transpose(1, 0, 2)
        x2 = _outffn_call(
            x2, y_mla.reshape(t, MLA_WIDTH), y_ret.reshape(t, RET_WIDTH), mod3,
            row(mla_out_norm[l]), row(post_norm_mix[l]), row(pre_norm_ffn[l]), row(post_norm_ffn[l]),
            _bf16(w_out[l]), wg3, wu3, _bf16(w_down[l]), seq)
    return x2.reshape(bsz, seq, d)
```

```python
import functools

import jax
import jax.numpy as jnp
import numpy as np
from jax import lax
from jax.experimental import pallas as pl
from jax.experimental.pallas import tpu as pltpu

MLA_HEADS = 8
MLA_NOPE = 64
MLA_ROPE = 32
MLA_V = 64
MLA_Q_RANK = 384
MLA_KV_RANK = 256
MLA_QK = MLA_NOPE + MLA_ROPE
RET_HEADS = 4
RET_QK = 64
RET_V = 128
MLA_WIDTH = MLA_HEADS * MLA_V
RET_WIDTH = RET_HEADS * RET_V
ROPE_BASE = 10000.0
EPS = 1e-6
LOG2E = 1.4426950408889634

LANES = 128
VMEM_LIMIT = 56 * 1024 * 1024

TM_IN = 512
TQ = 512
RET_C = 256
RET_T = 512
TM_OUT = 512
FF_CHUNK = 256

HEAD_LANES = LANES
ROPE_HALF = MLA_ROPE // 2
RET_HALF = RET_QK // 2
N_FREQ = ROPE_HALF + RET_HALF


def _bf16(x):
    return x.astype(jnp.bfloat16)


def _dot(a, b):
    return jnp.dot(a, b, preferred_element_type=jnp.float32)


def _dot_nt(a, b):
    return lax.dot_general(a, b, (((1,), (1,)), ((), ())), preferred_element_type=jnp.float32)


def _dot_tn(a, b):
    return lax.dot_general(a, b, (((0,), (0,)), ((), ())), preferred_element_type=jnp.float32)


def _rms(x, gain):
    return x * lax.rsqrt(jnp.mean(x * x, axis=-1, keepdims=True) + EPS) * gain


def _const_spec(shape):
    nd = len(shape)
    return pl.BlockSpec(shape, lambda *_: (0,) * nd, pipeline_mode=pl.Buffered(1))


def _mod_kernel(c_ref, w_ref, b_ref, o_ref):
    c = c_ref[...]
    a = _bf16(c * jax.nn.sigmoid(c))
    o_ref[...] = _dot(a, _bf16(w_ref[...])) + b_ref[...]


def _mod_call(c, w_ada, b_ada):
    bsz, d = c.shape
    n = w_ada.shape[1]
    tn = 1024
    return pl.pallas_call(
        _mod_kernel,
        grid=(n // tn,),
        in_specs=[pl.BlockSpec((bsz, d), lambda j: (0, 0)),
                  pl.BlockSpec((d, tn), lambda j: (0, j)),
                  pl.BlockSpec((1, tn), lambda j: (0, j))],
        out_specs=pl.BlockSpec((bsz, tn), lambda j: (0, j)),
        out_shape=jax.ShapeDtypeStruct((bsz, n), jnp.float32),
        compiler_params=pltpu.CompilerParams(dimension_semantics=("arbitrary",),
                                             vmem_limit_bytes=VMEM_LIMIT),
        name="adaln_mod",
    )(c, w_ada, b_ada.reshape(1, n))


_O_CQ = 0
_O_CKV = _O_CQ + MLA_Q_RANK
_O_KPE = _O_CKV + MLA_KV_RANK
_O_RQ = _O_KPE + HEAD_LANES
_O_RK = _O_RQ + RET_HEADS * RET_QK
_O_RV = _O_RK + RET_HEADS * RET_QK
_O_RG = _O_RV + RET_WIDTH
_IN_COLS_P = _O_RG + RET_WIDTH


def _swap_rope_halves(x):
    lane = lax.broadcasted_iota(jnp.int32, x.shape, 1)
    from_right = pltpu.roll(x, HEAD_LANES - ROPE_HALF, 1)
    from_left = pltpu.roll(x, ROPE_HALF, 1)
    return jnp.where(lane < MLA_NOPE + ROPE_HALF, from_right, from_left)


def _inproj_kernel(x_ref, sc_ref, sh_ref, pos_ref, inv_ref, gpre_ref, win_ref, gq_ref, wqb_ref,
                   gkv_ref, wk_ref, wvt_ref,
                   q_ref, k_ref, vt_ref, rq_ref, rk_ref, rv_ref, rg_ref):
    tm = x_ref.shape[0]
    x = x_ref[...]
    h = _bf16(_rms(x, gpre_ref[...]) * (1.0 + sc_ref[0]) + sh_ref[0])

    pos = pos_ref[0].astype(jnp.float32)
    ang = inv_ref[...] * pos
    cs = jnp.cos(ang)
    sn = jnp.sin(ang)
    c_m, c_r = cs[:ROPE_HALF], cs[ROPE_HALF:]
    s_m, s_r = sn[:ROPE_HALF], sn[ROPE_HALF:]
    ones = jnp.ones((MLA_NOPE, tm), jnp.float32)
    zeros_hi = jnp.zeros((HEAD_LANES - MLA_QK, tm), jnp.float32)
    zeros_lo = jnp.zeros((MLA_NOPE, tm), jnp.float32)
    cos_k = jnp.concatenate([ones, c_m, c_m, zeros_hi], axis=0).T
    sin_k = jnp.concatenate([zeros_lo, -s_m, s_m, zeros_hi], axis=0).T
    cos_r = jnp.concatenate([c_r] * RET_HEADS, axis=0).T
    sin_r = jnp.concatenate([s_r] * RET_HEADS, axis=0).T
    scale = MLA_QK ** -0.5 * LOG2E
    cos_q = cos_k * scale
    sin_q = sin_k * scale

    cq = _dot(h, win_ref[:, _O_CQ:_O_CQ + MLA_Q_RANK])
    q = _dot(_bf16(_rms(cq, gq_ref[...])), wqb_ref[...])
    for hd in range(MLA_HEADS):
        sl = slice(hd * HEAD_LANES, (hd + 1) * HEAD_LANES)
        qh = q[:, sl]
        q_ref[:, sl] = _bf16(qh * cos_q + _swap_rope_halves(qh) * sin_q)

    ckv = _dot(h, win_ref[:, _O_CKV:_O_CKV + MLA_KV_RANK])
    ckvn = _bf16(_rms(ckv, gkv_ref[...]))
    kpe = _dot(h, win_ref[:, _O_KPE:_O_KPE + HEAD_LANES])
    kpe = kpe * cos_k + _swap_rope_halves(kpe) * sin_k
    kn = _dot(ckvn, wk_ref[...])
    for hd in range(MLA_HEADS):
        sl = slice(hd * HEAD_LANES, (hd + 1) * HEAD_LANES)
        k_ref[:, sl] = _bf16(kn[:, sl] + kpe)
    vt_ref[0, 0] = _bf16(_dot_nt(wvt_ref[...], ckvn))

    for off, o_ref in ((_O_RQ, rq_ref), (_O_RK, rk_ref)):
        z = _dot(h, win_ref[:, off:off + 2 * LANES])
        z1, z2 = z[:, :LANES], z[:, LANES:]
        o_ref[:, :LANES] = _bf16(z1 * cos_r - z2 * sin_r)
        o_ref[:, LANES:] = _bf16(z1 * sin_r + z2 * cos_r)
    rv_ref[...] = _bf16(_dot(h, win_ref[:, _O_RV:_O_RV + RET_WIDTH]))
    rg_ref[...] = _bf16(_dot(h, win_ref[:, _O_RG:_O_RG + RET_WIDTH]))


def _inproj_call(x2, mod3, pos3, inv_col, gpre, win_p, gq, wqb_p, gkv, wk_p, wv_p, seq):
    t, d = x2.shape
    tm = TM_IN
    per_b = seq // tm
    tok = lambda n: pl.BlockSpec((tm, n), lambda i: (i, 0))
    mod_spec = lambda k: pl.BlockSpec((1, 1, d), lambda i: ((i // per_b) * 6 + k, 0, 0))
    outs = [(MLA_HEADS * HEAD_LANES), (MLA_HEADS * HEAD_LANES), None,
            RET_HEADS * RET_QK, RET_HEADS * RET_QK, RET_WIDTH, RET_WIDTH]
    vt_spec = pl.BlockSpec((1, 1, MLA_WIDTH, tm), lambda i: (i // per_b, i % per_b, 0, 0))
    vt_shape = jax.ShapeDtypeStruct((t // seq, per_b, MLA_WIDTH, tm), jnp.bfloat16)
    return pl.pallas_call(
        _inproj_kernel,
        grid=(t // tm,),
        in_specs=[tok(d), mod_spec(1), mod_spec(0),
                  pl.BlockSpec((1, 1, tm), lambda i: (i, 0, 0)),
                  _const_spec(inv_col.shape), _const_spec(gpre.shape), _const_spec(win_p.shape),
                  _const_spec(gq.shape), _const_spec(wqb_p.shape), _const_spec(gkv.shape),
                  _const_spec(wk_p.shape), _const_spec(wv_p.shape)],
        out_specs=[vt_spec if n is None else tok(n) for n in outs],
        out_shape=[vt_shape if n is None else jax.ShapeDtypeStruct((t, n), jnp.bfloat16) for n in outs],
        compiler_params=pltpu.CompilerParams(dimension_semantics=("arbitrary",),
                                             vmem_limit_bytes=VMEM_LIMIT),
        name="inproj",
    )(x2, mod3, mod3, pos3, inv_col, gpre, win_p, gq, wqb_p, gkv, wk_p, wv_p)


ONES_ROWS = 16


def _attn_kernel(q_ref, k_ref, vt_ref, bias_ref, o_ref, m_ref, acc_ref):
    qi = pl.program_id(2)
    tq = q_ref.shape[1]
    m_ref[...] = jnp.full(m_ref.shape, -jnp.inf, jnp.float32)
    acc_ref[...] = jnp.zeros(acc_ref.shape, jnp.float32)
    ones = jnp.ones((ONES_ROWS, tq), jnp.bfloat16)

    def step(j, masked):
        start = pl.multiple_of(j * tq, tq)
        for hd in range(2):
            sl = slice(hd * HEAD_LANES, (hd + 1) * HEAD_LANES)
            st = _dot_nt(k_ref[0, pl.ds(start, tq), sl], q_ref[0, :, sl])
            if masked:
                st = st + bias_ref[...]
            m_prev = m_ref[hd]
            m_new = jnp.maximum(m_prev, jnp.max(st, axis=0, keepdims=True))
            alpha = jnp.exp2(m_prev - m_new)
            p = _bf16(jnp.exp2(st - m_new))
            vt = vt_ref[0, j, hd * MLA_V:(hd + 1) * MLA_V, :]
            lhs = jnp.concatenate([vt, ones], axis=0)
            acc_ref[hd] = alpha * acc_ref[hd] + _dot(lhs, p)
            m_ref[hd] = m_new

    def body(j, carry):
        step(j, False)
        return carry

    lax.fori_loop(0, qi, body, 0)
    step(qi, True)

    outs = []
    for hd in range(2):
        acc = acc_ref[hd]
        outs.append(acc[:MLA_V] / acc[MLA_V:MLA_V + 1])
    o_ref[0] = _bf16(jnp.concatenate(outs, axis=0).T)


def _attn_call(q, k, vt):
    bsz, seq, _ = q.shape
    tq = TQ
    pairs = MLA_HEADS // 2
    idx = jnp.arange(tq)
    bias = jnp.where(idx[:, None] <= idx[None, :], 0.0, -jnp.inf).astype(jnp.float32)
    return pl.pallas_call(
        _attn_kernel,
        grid=(bsz, pairs, seq // tq),
        in_specs=[pl.BlockSpec((1, tq, 2 * HEAD_LANES), lambda b, p, i: (b, i, p)),
                  pl.BlockSpec((1, seq, 2 * HEAD_LANES), lambda b, p, i: (b, 0, p)),
                  pl.BlockSpec((1, seq // tq, 2 * MLA_V, tq), lambda b, p, i: (b, 0, p, 0)),
                  _const_spec(bias.shape)],
        out_specs=pl.BlockSpec((1, tq, LANES), lambda b, p, i: (b, i, p)),
        out_shape=jax.ShapeDtypeStruct((bsz, seq, MLA_WIDTH), jnp.bfloat16),
        scratch_shapes=[pltpu.VMEM((2, 1, tq), jnp.float32),
                        pltpu.VMEM((2, MLA_V + ONES_ROWS, tq), jnp.float32)],
        compiler_params=pltpu.CompilerParams(
            dimension_semantics=("arbitrary", "arbitrary", "arbitrary"),
            vmem_limit_bytes=VMEM_LIMIT),
        name="mla_attn",
    )(q, k, vt, bias)


def _ret_kernel(q_ref, k_ref, v_ref, g_ref, dec_ref, wq_ref, wk_ref, cd_ref, gain_ref,
                o_ref, state_ref):
    @pl.when(pl.program_id(1) == 0)
    def _():
        state_ref[...] = jnp.zeros(state_ref.shape, jnp.float32)

    c = RET_C
    lane = lax.broadcasted_iota(jnp.int32, (1, 2 * LANES), 1)
    for ci in range(q_ref.shape[1] // c):
        rows = slice(ci * c, (ci + 1) * c)
        q = q_ref[0, rows, :].astype(jnp.float32)
        k = k_ref[0, rows, :]
        kf = k.astype(jnp.float32)
        for hd in range(RET_HEADS):
            vh = v_ref[0, rows, hd * RET_V:(hd + 1) * RET_V]
            head = ((lane % LANES) // RET_HALF) == hd
            qm = jnp.where(head, q, 0.0)
            sc = _dot_nt(_bf16(qm), k) * dec_ref[hd]
            inner = _dot(_bf16(sc), vh)
            st = state_ref[hd]
            cross = _dot(_bf16(qm * wq_ref[hd]), _bf16(st))
            u = _dot_tn(_bf16(kf * wk_ref[hd]), vh)
            state_ref[hd] = st * cd_ref[hd] + u
            o = inner + cross
            mu = jnp.mean(o, axis=-1, keepdims=True)
            var = jnp.mean(jnp.square(o - mu), axis=-1, keepdims=True)
            cols = slice(hd * RET_V, (hd + 1) * RET_V)
            on = (o - mu) * lax.rsqrt(var + EPS) * gain_ref[:, cols]
            g = g_ref[0, rows, cols].astype(jnp.float32)
            o_ref[0, rows, cols] = _bf16(g * jax.nn.sigmoid(g) * on)


def _ret_tables():
    c = RET_C
    f32 = jnp.float32
    log_gamma = jnp.log(1.0 - 2.0 ** (-5.0 - jnp.arange(RET_HEADS, dtype=f32)))
    idx = jnp.arange(c)
    rel = idx[:, None] - idx[None, :]
    k_scale = RET_QK ** -0.5
    dec = jnp.where(rel >= 0, jnp.exp(log_gamma[:, None, None] * jnp.maximum(rel, 0).astype(f32)), 0.0)
    dec = dec * k_scale
    w_q = jnp.exp(log_gamma[:, None] * (idx + 1).astype(f32))[:, :, None]
    w_k = (jnp.exp(log_gamma[:, None] * (c - 1 - idx).astype(f32)) * k_scale)[:, :, None]
    cd = jnp.exp(log_gamma * c)[:, None, None]
    w_q = jnp.broadcast_to(w_q, (RET_HEADS, c, 2 * LANES))
    w_k = jnp.broadcast_to(w_k, (RET_HEADS, c, 2 * LANES))
    cd = jnp.broadcast_to(cd, (RET_HEADS, 1, LANES))
    return dec, w_q, w_k, cd


def _ret_call(rq, rk, rv, rg, gn_gain):
    bsz, seq, _ = rq.shape
    tt = RET_T
    dec, w_q, w_k, cd = _ret_tables()
    tok = lambda n: pl.BlockSpec((1, tt, n), lambda b, i: (b, i, 0))
    return pl.pallas_call(
        _ret_kernel,
        grid=(bsz, seq // tt),
        in_specs=[tok(2 * LANES), tok(2 * LANES), tok(RET_WIDTH), tok(RET_WIDTH),
                  _const_spec(dec.shape), _const_spec(w_q.shape), _const_spec(w_k.shape),
                  _const_spec(cd.shape), _const_spec(gn_gain.shape)],
        out_specs=tok(RET_WIDTH),
        out_shape=jax.ShapeDtypeStruct((bsz, seq, RET_WIDTH), jnp.bfloat16),
        scratch_shapes=[pltpu.VMEM((RET_HEADS, 2 * LANES, RET_V), jnp.float32)],
        compiler_params=pltpu.CompilerParams(dimension_semantics=("arbitrary", "arbitrary"),
                                             vmem_limit_bytes=VMEM_LIMIT),
        name="retention",
    )(rq, rk, rv, rg, dec, w_q, w_k, cd, gn_gain)


def _outffn_kernel(x_ref, ym_ref, yr_ref, g1_ref, sh2_ref, sc2_ref, g2_ref,
                   gmla_ref, gpost_ref, gpre2_ref, gpost2_ref,
                   wout_ref, wg_ref, wu_ref, wd_ref, o_ref, h_ref, a_ref):
    ymn = _bf16(_rms(ym_ref[...].astype(jnp.float32), gmla_ref[...]))
    mix = _dot(ymn, wout_ref[:MLA_WIDTH, :]) + _dot(yr_ref[...], wout_ref[MLA_WIDTH:, :])
    x1 = x_ref[...] + g1_ref[0] * _rms(mix, gpost_ref[...])
    o_ref[...] = x1
    h_ref[...] = _bf16(_rms(x1, gpre2_ref[...]) * (1.0 + sc2_ref[0]) + sh2_ref[0])

    n_chunks = wg_ref.shape[0]
    for ci in range(n_chunks):
        h = h_ref[...]
        g = _dot(h, wg_ref[ci])
        u = _dot(h, wu_ref[ci])
        a_ref[:, ci * FF_CHUNK:(ci + 1) * FF_CHUNK] = _bf16(g * jax.nn.sigmoid(g) * u)
    f = _dot(a_ref[...], wd_ref[...])
    o_ref[...] = o_ref[...] + g2_ref[0] * _rms(f, gpost2_ref[...])


def _outffn_call(x2, ym, yr, mod3, gmla, gpost, gpre2, gpost2, wout, wg3, wu3, wd, seq):
    t, d = x2.shape
    tm = TM_OUT
    per_b = seq // tm
    d_ff = wd.shape[0]
    tok = lambda n: pl.BlockSpec((tm, n), lambda i: (i, 0))
    mod_spec = lambda k: pl.BlockSpec((1, 1, d), lambda i: ((i // per_b) * 6 + k, 0, 0))
    return pl.pallas_call(
        _outffn_kernel,
        grid=(t // tm,),
        in_specs=[tok(d), tok(MLA_WIDTH), tok(RET_WIDTH),
                  mod_spec(2), mod_spec(3), mod_spec(4), mod_spec(5),
                  _const_spec(gmla.shape), _const_spec(gpost.shape), _const_spec(gpre2.shape),
                  _const_spec(gpost2.shape), _const_spec(wout.shape), _const_spec(wg3.shape),
                  _const_spec(wu3.shape), _const_spec(wd.shape)],
        out_specs=tok(d),
        out_shape=jax.ShapeDtypeStruct((t, d), jnp.float32),
        scratch_shapes=[pltpu.VMEM((tm, d), jnp.bfloat16), pltpu.VMEM((tm, d_ff), jnp.bfloat16)],
        compiler_params=pltpu.CompilerParams(dimension_semantics=("arbitrary",),
                                             vmem_limit_bytes=VMEM_LIMIT),
        name="outproj_ffn",
    )(x2, ym, yr, mod3, mod3, mod3, mod3, gmla, gpost, gpre2, gpost2, wout, wg3, wu3, wd)


def _relayout_w_in(w_in):
    d = w_in.shape[0]
    o = np.cumsum([0, MLA_Q_RANK, MLA_KV_RANK, MLA_ROPE, RET_HEADS * RET_QK, RET_HEADS * RET_QK,
                   RET_WIDTH, RET_WIDTH])
    cq, ckv, kpe, rq, rk, rv, rg = [w_in[:, o[i]:o[i + 1]] for i in range(7)]
    kpe = jnp.concatenate([jnp.zeros((d, MLA_NOPE), w_in.dtype), kpe,
                           jnp.zeros((d, HEAD_LANES - MLA_QK), w_in.dtype)], axis=1)

    def halves_first(w):
        return w.reshape(d, RET_HEADS, 2, RET_HALF).transpose(0, 2, 1, 3).reshape(d, RET_HEADS * RET_QK)

    return _bf16(jnp.concatenate([cq, ckv, kpe, halves_first(rq), halves_first(rk), rv, rg], axis=1))


def _relayout_w_q_b(w):
    r = w.shape[0]
    w = w.reshape(r, MLA_HEADS, MLA_QK)
    w = jnp.pad(w, ((0, 0), (0, 0), (0, HEAD_LANES - MLA_QK)))
    return _bf16(w.reshape(r, MLA_HEADS * HEAD_LANES))


def _relayout_w_kv_b(w):
    r = w.shape[0]
    w = w.reshape(r, MLA_HEADS, MLA_NOPE + MLA_V)
    wk = jnp.pad(w[..., :MLA_NOPE], ((0, 0), (0, 0), (0, HEAD_LANES - MLA_NOPE)))
    wv = w[..., MLA_NOPE:]
    return _bf16(wk.reshape(r, MLA_HEADS * HEAD_LANES)), _bf16(wv.reshape(r, MLA_WIDTH).T)


def _rope_inv_col():
    inv_m = ROPE_BASE ** (-jnp.arange(0, MLA_ROPE, 2, dtype=jnp.float32) / MLA_ROPE)
    inv_r = ROPE_BASE ** (-jnp.arange(0, RET_QK, 2, dtype=jnp.float32) / RET_QK)
    return jnp.concatenate([inv_m, inv_r]).reshape(N_FREQ, 1)


def kernel(x, c, positions, w_ada, b_ada, pre_norm_mix, w_in, q_a_norm, w_q_b, kv_a_norm, w_kv_b,
           mla_out_norm, ret_gn_gain, w_out, post_norm_mix, pre_norm_ffn, w_gate, w_up, w_down,
           post_norm_ffn):
    bsz, seq, d = x.shape
    t = bsz * seq
    depth = w_ada.shape[0]
    d_ff = w_gate.shape[-1]
    assert seq % TM_IN == 0 and seq % TQ == 0 and seq % RET_T == 0 and seq % TM_OUT == 0
    assert RET_T % RET_C == 0 and d_ff % FF_CHUNK == 0
    assert TM_IN == TQ

    row = lambda g: g.reshape(1, -1)
    pos3 = positions.reshape(t // TM_IN, 1, TM_IN)
    inv_col = _rope_inv_col()
    x2 = x.reshape(t, d)
    for l in range(depth):
        mod3 = _mod_call(c, w_ada[l], b_ada[l]).reshape(bsz * 6, 1, d)
        wk_p, wv_p = _relayout_w_kv_b(w_kv_b[l])
        q, k, v, rq, rk, rv, rg = _inproj_call(
            x2, mod3, pos3, inv_col, row(pre_norm_mix[l]), _relayout_w_in(w_in[l]),
            row(q_a_norm[l]), _relayout_w_q_b(w_q_b[l]), row(kv_a_norm[l]), wk_p, wv_p, seq)
        b3 = lambda a: a.reshape(bsz, seq, a.shape[-1])
        y_mla = _attn_call(b3(q), b3(k), v)
        y_ret = _ret_call(b3(rq), b3(rk), b3(rv), b3(rg), row(ret_gn_gain[l]))
        n_ch = d_ff // FF_CHUNK
        wg3 = _bf16(w_gate[l]).reshape(d, n_ch, FF_CHUNK).transpose(1, 0, 2)
        wu3 = _bf16(w_up[l]).reshape(d, n_ch, FF_CHUNK).transpose(1, 0, 2)
        x2 = _outffn_call(
            x2, y_mla.reshape(t, MLA_WIDTH), y_ret.reshape(t, RET_WIDTH), mod3,
            row(mla_out_norm[l]), row(post_norm_mix[l]), row(pre_norm_ffn[l]), row(post_norm_ffn[l]),
            _bf16(w_out[l]), wg3, wu3, _bf16(w_down[l]), seq)
    return x2.reshape(bsz, seq, d)
```

```python
import functools

import jax
import jax.numpy as jnp
import numpy as np
from jax import lax
from jax.experimental import pallas as pl
from jax.experimental.pallas import tpu as pltpu

MLA_HEADS = 8
MLA_NOPE = 64
MLA_ROPE = 32
MLA_V = 64
MLA_Q_RANK = 384
MLA_KV_RANK = 256
MLA_QK = MLA_NOPE + MLA_ROPE
RET_HEADS = 4
RET_QK = 64
RET_V = 128
MLA_WIDTH = MLA_HEADS * MLA_V
RET_WIDTH = RET_HEADS * RET_V
ROPE_BASE = 10000.0
EPS = 1e-6
LOG2E = 1.4426950408889634

LANES = 128
VMEM_LIMIT = 56 * 1024 * 1024

TM_IN = 512
TQ = 512
RET_C = 256
RET_T = 512
TM_OUT = 512
FF_CHUNK = 256

HEAD_LANES = LANES
ROPE_HALF = MLA_ROPE // 2
RET_HALF = RET_QK // 2
N_FREQ = ROPE_HALF + RET_HALF


def _bf16(x):
    return x.astype(jnp.bfloat16)


def _dot(a, b):
    return jnp.dot(a, b, preferred_element_type=jnp.float32)


def _dot_nt(a, b):
    return lax.dot_general(a, b, (((1,), (1,)), ((), ())), preferred_element_type=jnp.float32)


def _dot_tn(a, b):
    return lax.dot_general(a, b, (((0,), (0,)), ((), ())), preferred_element_type=jnp.float32)


def _rms(x, gain):
    return x * lax.rsqrt(jnp.mean(x * x, axis=-1, keepdims=True) + EPS) * gain


def _const_spec(shape):
    nd = len(shape)
    return pl.BlockSpec(shape, lambda *_: (0,) * nd, pipeline_mode=pl.Buffered(1))


def _mod_kernel(c_ref, w_ref, b_ref, o_ref):
    c = c_ref[...]
    a = _bf16(c * jax.nn.sigmoid(c))
    o_ref[...] = _dot(a, _bf16(w_ref[...])) + b_ref[...]


def _mod_call(c, w_ada, b_ada):
    bsz, d = c.shape
    n = w_ada.shape[1]
    tn = 1024
    return pl.pallas_call(
        _mod_kernel,
        grid=(n // tn,),
        in_specs=[pl.BlockSpec((bsz, d), lambda j: (0, 0)),
                  pl.BlockSpec((d, tn), lambda j: (0, j)),
                  pl.BlockSpec((1, tn), lambda j: (0, j))],
        out_specs=pl.BlockSpec((bsz, tn), lambda j: (0, j)),
        out_shape=jax.ShapeDtypeStruct((bsz, n), jnp.float32),
        compiler_params=pltpu.CompilerParams(dimension_semantics=("arbitrary",),
                                             vmem_limit_bytes=VMEM_LIMIT),
        name="adaln_mod",
    )(c, w_ada, b_ada.reshape(1, n))


_O_CQ = 0
_O_CKV = _O_CQ + MLA_Q_RANK
_O_KPE = _O_CKV + MLA_KV_RANK
_O_RQ = _O_KPE + HEAD_LANES
_O_RK = _O_RQ + RET_HEADS * RET_QK
_O_RV = _O_RK + RET_HEADS * RET_QK
_O_RG = _O_RV + RET_WIDTH
_IN_COLS_P = _O_RG + RET_WIDTH


def _swap_rope_halves(x):
    lane = lax.broadcasted_iota(jnp.int32, x.shape, 1)
    from_right = pltpu.roll(x, HEAD_LANES - ROPE_HALF, 1)
    from_left = pltpu.roll(x, ROPE_HALF, 1)
    return jnp.where(lane < MLA_NOPE + ROPE_HALF, from_right, from_left)


def _inproj_kernel(x_ref, sc_ref, sh_ref, pos_ref, inv_ref, gpre_ref, win_ref, gq_ref, wqb_ref,
                   gkv_ref, wk_ref, wvt_ref,
                   q_ref, k_ref, vt_ref, rq_ref, rk_ref, rv_ref, rg_ref):
    tm = x_ref.shape[0]
    x = x_ref[...]
    h = _bf16(_rms(x, gpre_ref[...]) * (1.0 + sc_ref[0]) + sh_ref[0])

    pos = pos_ref[0].astype(jnp.float32)
    ang = inv_ref[...] * pos
    cs = jnp.cos(ang)
    sn = jnp.sin(ang)
    c_m, c_r = cs[:ROPE_HALF], cs[ROPE_HALF:]
    s_m, s_r = sn[:ROPE_HALF], sn[ROPE_HALF:]
    ones = jnp.ones((MLA_NOPE, tm), jnp.float32)
    zeros_hi = jnp.zeros((HEAD_LANES - MLA_QK, tm), jnp.float32)
    zeros_lo = jnp.zeros((MLA_NOPE, tm), jnp.float32)
    cos_k = jnp.concatenate([ones, c_m, c_m, zeros_hi], axis=0).T
    sin_k = jnp.concatenate([zeros_lo, -s_m, s_m, zeros_hi], axis=0).T
    cos_r = jnp.concatenate([c_r] * RET_HEADS, axis=0).T
    sin_r = jnp.concatenate([s_r] * RET_HEADS, axis=0).T
    scale = MLA_QK ** -0.5 * LOG2E
    cos_q = cos_k * scale
    sin_q = sin_k * scale

    cq = _dot(h, win_ref[:, _O_CQ:_O_CQ + MLA_Q_RANK])
    q = _dot(_bf16(_rms(cq, gq_ref[...])), wqb_ref[...])
    for hd in range(MLA_HEADS):
        sl = slice(hd * HEAD_LANES, (hd + 1) * HEAD_LANES)
        qh = q[:, sl]
        q_ref[:, sl] = _bf16(qh * cos_q + _swap_rope_halves(qh) * sin_q)

    ckv = _dot(h, win_ref[:, _O_CKV:_O_CKV + MLA_KV_RANK])
    ckvn = _bf16(_rms(ckv, gkv_ref[...]))
    kpe = _dot(h, win_ref[:, _O_KPE:_O_KPE + HEAD_LANES])
    kpe = kpe * cos_k + _swap_rope_halves(kpe) * sin_k
    kn = _dot(ckvn, wk_ref[...])
    for hd in range(MLA_HEADS):
        sl = slice(hd * HEAD_LANES, (hd + 1) * HEAD_LANES)
        k_ref[:, sl] = _bf16(kn[:, sl] + kpe)
    vt_ref[0, 0] = _bf16(_dot_nt(wvt_ref[...], ckvn))

    for off, o_ref in ((_O_RQ, rq_ref), (_O_RK, rk_ref)):
        z = _dot(h, win_ref[:, off:off + 2 * LANES])
        z1, z2 = z[:, :LANES], z[:, LANES:]
        o_ref[:, :LANES] = _bf16(z1 * cos_r - z2 * sin_r)
        o_ref[:, LANES:] = _bf16(z1 * sin_r + z2 * cos_r)
    rv_ref[...] = _bf16(_dot(h, win_ref[:, _O_RV:_O_RV + RET_WIDTH]))
    rg_ref[...] = _bf16(_dot(h, win_ref[:, _O_RG:_O_RG + RET_WIDTH]))


def _inproj_call(x2, mod3, pos3, inv_col, gpre, win_p, gq, wqb_p, gkv, wk_p, wv_p, seq):
    t, d = x2.shape
    tm = TM_IN
    per_b = seq // tm
    tok = lambda n: pl.BlockSpec((tm, n), lambda i: (i, 0))
    mod_spec = lambda k: pl.BlockSpec((1, 1, d), lambda i: ((i // per_b) * 6 + k, 0, 0))
    outs = [(MLA_HEADS * HEAD_LANES), (MLA_HEADS * HEAD_LANES), None,
            RET_HEADS * RET_QK, RET_HEADS * RET_QK, RET_WIDTH, RET_WIDTH]
    vt_spec = pl.BlockSpec((1, 1, MLA_WIDTH, tm), lambda i: (i // per_b, i % per_b, 0, 0))
    vt_shape = jax.ShapeDtypeStruct((t // seq, per_b, MLA_WIDTH, tm), jnp.bfloat16)
    return pl.pallas_call(
        _inproj_kernel,
        grid=(t // tm,),
        in_specs=[tok(d), mod_spec(1), mod_spec(0),
                  pl.BlockSpec((1, 1, tm), lambda i: (i, 0, 0)),
                  _const_spec(inv_col.shape), _const_spec(gpre.shape), _const_spec(win_p.shape),
                  _const_spec(gq.shape), _const_spec(wqb_p.shape), _const_spec(gkv.shape),
                  _const_spec(wk_p.shape), _const_spec(wv_p.shape)],
        out_specs=[vt_spec if n is None else tok(n) for n in outs],
        out_shape=[vt_shape if n is None else jax.ShapeDtypeStruct((t, n), jnp.bfloat16) for n in outs],
        compiler_params=pltpu.CompilerParams(dimension_semantics=("arbitrary",),
                                             vmem_limit_bytes=VMEM_LIMIT),
        name="inproj",
    )(x2, mod3, mod3, pos3, inv_col, gpre, win_p, gq, wqb_p, gkv, wk_p, wv_p)


Q_SUB = 256
ONES_ROWS = 16


def _attn_kernel(q_ref, k_ref, vt_ref, bias_ref, o_ref, m_ref, acc_ref, sa_ref, sb_ref, mxa_ref, mxb_ref):
    qi = pl.program_id(2)
    tq = q_ref.shape[1]
    m_ref[...] = jnp.full(m_ref.shape, -jnp.inf, jnp.float32)
    acc_ref[...] = jnp.zeros(acc_ref.shape, jnp.float32)
    ones = jnp.ones((ONES_ROWS, tq), jnp.bfloat16)
    subs = [(hd, slice(qh * Q_SUB, (qh + 1) * Q_SUB)) for hd in range(2) for qh in range(tq // Q_SUB)]

    def produce(j, s_ref, mx_ref):
        start = pl.multiple_of(j * tq, tq)
        for i, (hd, qs) in enumerate(subs):
            sl = slice(hd * HEAD_LANES, (hd + 1) * HEAD_LANES)
            st = _dot_nt(k_ref[0, pl.ds(start, tq), sl], q_ref[0, qs, sl])
            s_ref[i] = st
            mx_ref[i] = jnp.max(st, axis=0, keepdims=True)

    def consume(j, s_ref, mx_ref, masked):
        for i, (hd, qs) in enumerate(subs):
            vt = vt_ref[0, j, hd * MLA_V:(hd + 1) * MLA_V, :]
            lhs = jnp.concatenate([vt, ones], axis=0)
            if masked:
                st = s_ref[i] + bias_ref[:, qs]
                m_cur = jnp.max(st, axis=0, keepdims=True)
            else:
                st = s_ref[i]
                m_cur = mx_ref[i]
            m_prev = m_ref[hd, :, qs]
            m_new = jnp.maximum(m_prev, m_cur)
            alpha = jnp.exp2(m_prev - m_new)
            p = _bf16(jnp.exp2(st - m_new))
            acc_ref[hd, :, qs] = alpha * acc_ref[hd, :, qs] + _dot(lhs, p)
            m_ref[hd, :, qs] = m_new

    buf_a, buf_b = (sa_ref, mxa_ref), (sb_ref, mxb_ref)
    produce(0, *buf_a)

    def pair_body(i, carry):
        j = 2 * i
        produce(j + 1, *buf_b)
        consume(j, *buf_a, False)
        produce(j + 2, *buf_a)
        consume(j + 1, *buf_b, False)
        return carry

    lax.fori_loop(0, qi // 2, pair_body, 0)

    @pl.when(qi % 2 == 0)
    def _():
        consume(qi, *buf_a, True)

    @pl.when(qi % 2 == 1)
    def _():
        produce(qi, *buf_b)
        consume(qi - 1, *buf_a, False)
        consume(qi, *buf_b, True)

    outs = []
    for hd in range(2):
        acc = acc_ref[hd]
        outs.append(acc[:MLA_V] / acc[MLA_V:MLA_V + 1])
    o_ref[0] = _bf16(jnp.concatenate(outs, axis=0).T)


def _attn_call(q, k, vt):
    bsz, seq, _ = q.shape
    tq = TQ
    pairs = MLA_HEADS // 2
    n_sub = 2 * (tq // Q_SUB)
    idx = jnp.arange(tq)
    bias = jnp.where(idx[:, None] <= idx[None, :], 0.0, -jnp.inf).astype(jnp.float32)
    return pl.pallas_call(
        _attn_kernel,
        grid=(bsz, pairs, seq // tq),
        in_specs=[pl.BlockSpec((1, tq, 2 * HEAD_LANES), lambda b, p, i: (b, i, p)),
                  pl.BlockSpec((1, seq, 2 * HEAD_LANES), lambda b, p, i: (b, 0, p)),
                  pl.BlockSpec((1, seq // tq, 2 * MLA_V, tq), lambda b, p, i: (b, 0, p, 0)),
                  _const_spec(bias.shape)],
        out_specs=pl.BlockSpec((1, tq, LANES), lambda b, p, i: (b, i, p)),
        out_shape=jax.ShapeDtypeStruct((bsz, seq, MLA_WIDTH), jnp.bfloat16),
        scratch_shapes=[pltpu.VMEM((2, 1, tq), jnp.float32),
                        pltpu.VMEM((2, MLA_V + ONES_ROWS, tq), jnp.float32)]
        + [pltpu.VMEM((n_sub, tq, Q_SUB), jnp.float32)] * 2
        + [pltpu.VMEM((n_sub, 1, Q_SUB), jnp.float32)] * 2,
        compiler_params=pltpu.CompilerParams(
            dimension_semantics=("arbitrary", "arbitrary", "arbitrary"),
            vmem_limit_bytes=VMEM_LIMIT),
        name="mla_attn",
    )(q, k, vt, bias)


def _ret_kernel(q_ref, k_ref, v_ref, g_ref, dec_ref, wq_ref, wk_ref, cd_ref, gain_ref,
                o_ref, state_ref):
    @pl.when(pl.program_id(1) == 0)
    def _():
        state_ref[...] = jnp.zeros(state_ref.shape, jnp.float32)

    c = RET_C
    lane = lax.broadcasted_iota(jnp.int32, (1, 2 * LANES), 1)
    for ci in range(q_ref.shape[1] // c):
        rows = slice(ci * c, (ci + 1) * c)
        q = q_ref[0, rows, :].astype(jnp.float32)
        k = k_ref[0, rows, :]
        kf = k.astype(jnp.float32)
        for hd in range(RET_HEADS):
            vh = v_ref[0, rows, hd * RET_V:(hd + 1) * RET_V]
            head = ((lane % LANES) // RET_HALF) == hd
            qm = jnp.where(head, q, 0.0)
            sc = _dot_nt(_bf16(qm), k) * dec_ref[hd]
            inner = _dot(_bf16(sc), vh)
            st = state_ref[hd]
            cross = _dot(_bf16(qm * wq_ref[hd]), _bf16(st))
            u = _dot_tn(_bf16(kf * wk_ref[hd]), vh)
            state_ref[hd] = st * cd_ref[hd] + u
            o = inner + cross
            mu = jnp.mean(o, axis=-1, keepdims=True)
            var = jnp.mean(jnp.square(o - mu), axis=-1, keepdims=True)
            cols = slice(hd * RET_V, (hd + 1) * RET_V)
            on = (o - mu) * lax.rsqrt(var + EPS) * gain_ref[:, cols]
            g = g_ref[0, rows, cols].astype(jnp.float32)
            o_ref[0, rows, cols] = _bf16(g * jax.nn.sigmoid(g) * on)


def _ret_tables():
    c = RET_C
    f32 = jnp.float32
    log_gamma = jnp.log(1.0 - 2.0 ** (-5.0 - jnp.arange(RET_HEADS, dtype=f32)))
    idx = jnp.arange(c)
    rel = idx[:, None] - idx[None, :]
    k_scale = RET_QK ** -0.5
    dec = jnp.where(rel >= 0, jnp.exp(log_gamma[:, None, None] * jnp.maximum(rel, 0).astype(f32)), 0.0)
    dec = dec * k_scale
    w_q = jnp.exp(log_gamma[:, None] * (idx + 1).astype(f32))[:, :, None]
    w_k = (jnp.exp(log_gamma[:, None] * (c - 1 - idx).astype(f32)) * k_scale)[:, :, None]
    cd = jnp.exp(log_gamma * c)[:, None, None]
    w_q = jnp.broadcast_to(w_q, (RET_HEADS, c, 2 * LANES))
    w_k = jnp.broadcast_to(w_k, (RET_HEADS, c, 2 * LANES))
    cd = jnp.broadcast_to(cd, (RET_HEADS, 1, LANES))
    return dec, w_q, w_k, cd


def _ret_call(rq, rk, rv, rg, gn_gain):
    bsz, seq, _ = rq.shape
    tt = RET_T
    dec, w_q, w_k, cd = _ret_tables()
    tok = lambda n: pl.BlockSpec((1, tt, n), lambda b, i: (b, i, 0))
    return pl.pallas_call(
        _ret_kernel,
        grid=(bsz, seq // tt),
        in_specs=[tok(2 * LANES), tok(2 * LANES), tok(RET_WIDTH), tok(RET_WIDTH),
                  _const_spec(dec.shape), _const_spec(w_q.shape), _const_spec(w_k.shape),
                  _const_spec(cd.shape), _const_spec(gn_gain.shape)],
        out_specs=tok(RET_WIDTH),
        out_shape=jax.ShapeDtypeStruct((bsz, seq, RET_WIDTH), jnp.bfloat16),
        scratch_shapes=[pltpu.VMEM((RET_HEADS, 2 * LANES, RET_V), jnp.float32)],
        compiler_params=pltpu.CompilerParams(dimension_semantics=("arbitrary", "arbitrary"),
                                             vmem_limit_bytes=VMEM_LIMIT),
        name="retention",
    )(rq, rk, rv, rg, dec, w_q, w_k, cd, gn_gain)


def _outffn_kernel(x_ref, ym_ref, yr_ref, g1_ref, sh2_ref, sc2_ref, g2_ref,
                   gmla_ref, gpost_ref, gpre2_ref, gpost2_ref,
                   wout_ref, wg_ref, wu_ref, wd_ref, o_ref, h_ref, a_ref):
    ymn = _bf16(_rms(ym_ref[...].astype(jnp.float32), gmla_ref[...]))
    mix = _dot(ymn, wout_ref[:MLA_WIDTH, :]) + _dot(yr_ref[...], wout_ref[MLA_WIDTH:, :])
    x1 = x_ref[...] + g1_ref[0] * _rms(mix, gpost_ref[...])
    o_ref[...] = x1
    h_ref[...] = _bf16(_rms(x1, gpre2_ref[...]) * (1.0 + sc2_ref[0]) + sh2_ref[0])

    n_chunks = wg_ref.shape[0]
    for ci in range(n_chunks):
        h = h_ref[...]
        g = _dot(h, wg_ref[ci])
        u = _dot(h, wu_ref[ci])
        a_ref[:, ci * FF_CHUNK:(ci + 1) * FF_CHUNK] = _bf16(g * jax.nn.sigmoid(g) * u)
    f = _dot(a_ref[...], wd_ref[...])
    o_ref[...] = o_ref[...] + g2_ref[0] * _rms(f, gpost2_ref[...])


def _outffn_call(x2, ym, yr, mod3, gmla, gpost, gpre2, gpost2, wout, wg3, wu3, wd, seq):
    t, d = x2.shape
    tm = TM_OUT
    per_b = seq // tm
    d_ff = wd.shape[0]
    tok = lambda n: pl.BlockSpec((tm, n), lambda i: (i, 0))
    mod_spec = lambda k: pl.BlockSpec((1, 1, d), lambda i: ((i // per_b) * 6 + k, 0, 0))
    return pl.pallas_call(
        _outffn_kernel,
        grid=(t // tm,),
        in_specs=[tok(d), tok(MLA_WIDTH), tok(RET_WIDTH),
                  mod_spec(2), mod_spec(3), mod_spec(4), mod_spec(5),
                  _const_spec(gmla.shape), _const_spec(gpost.shape), _const_spec(gpre2.shape),
                  _const_spec(gpost2.shape), _const_spec(wout.shape), _const_spec(wg3.shape),
                  _const_spec(wu3.shape), _const_spec(wd.shape)],
        out_specs=tok(d),
        out_shape=jax.ShapeDtypeStruct((t, d), jnp.float32),
        scratch_shapes=[pltpu.VMEM((tm, d), jnp.bfloat16), pltpu.VMEM((tm, d_ff), jnp.bfloat16)],
        compiler_params=pltpu.CompilerParams(dimension_semantics=("arbitrary",),
                                             vmem_limit_bytes=VMEM_LIMIT),
        name="outproj_ffn",
    )(x2, ym, yr, mod3, mod3, mod3, mod3, gmla, gpost, gpre2, gpost2, wout, wg3, wu3, wd)


def _relayout_w_in(w_in):
    d = w_in.shape[0]
    o = np.cumsum([0, MLA_Q_RANK, MLA_KV_RANK, MLA_ROPE, RET_HEADS * RET_QK, RET_HEADS * RET_QK,
                   RET_WIDTH, RET_WIDTH])
    cq, ckv, kpe, rq, rk, rv, rg = [w_in[:, o[i]:o[i + 1]] for i in range(7)]
    kpe = jnp.concatenate([jnp.zeros((d, MLA_NOPE), w_in.dtype), kpe,
                           jnp.zeros((d, HEAD_LANES - MLA_QK), w_in.dtype)], axis=1)

    def halves_first(w):
        return w.reshape(d, RET_HEADS, 2, RET_HALF).transpose(0, 2, 1, 3).reshape(d, RET_HEADS * RET_QK)

    return _bf16(jnp.concatenate([cq, ckv, kpe, halves_first(rq), halves_first(rk), rv, rg], axis=1))


def _relayout_w_q_b(w):
    r = w.shape[0]
    w = w.reshape(r, MLA_HEADS, MLA_QK)
    w = jnp.pad(w, ((0, 0), (0, 0), (0, HEAD_LANES - MLA_QK)))
    return _bf16(w.reshape(r, MLA_HEADS * HEAD_LANES))


def _relayout_w_kv_b(w):
    r = w.shape[0]
    w = w.reshape(r, MLA_HEADS, MLA_NOPE + MLA_V)
    wk = jnp.pad(w[..., :MLA_NOPE], ((0, 0), (0, 0), (0, HEAD_LANES - MLA_NOPE)))
    wv = w[..., MLA_NOPE:]
    return _bf16(wk.reshape(r, MLA_HEADS * HEAD_LANES)), _bf16(wv.reshape(r, MLA_WIDTH).T)


def _rope_inv_col():
    inv_m = ROPE_BASE ** (-jnp.arange(0, MLA_ROPE, 2, dtype=jnp.float32) / MLA_ROPE)
    inv_r = ROPE_BASE ** (-jnp.arange(0, RET_QK, 2, dtype=jnp.float32) / RET_QK)
    return jnp.concatenate([inv_m, inv_r]).reshape(N_FREQ, 1)


def kernel(x, c, positions, w_ada, b_ada, pre_norm_mix, w_in, q_a_norm, w_q_b, kv_a_norm, w_kv_b,
           mla_out_norm, ret_gn_gain, w_out, post_norm_mix, pre_norm_ffn, w_gate, w_up, w_down,
           post_norm_ffn):
    bsz, seq, d = x.shape
    t = bsz * seq
    depth = w_ada.shape[0]
    d_ff = w_gate.shape[-1]
    assert seq % TM_IN == 0 and seq % TQ == 0 and seq % RET_T == 0 and seq % TM_OUT == 0
    assert RET_T % RET_C == 0 and d_ff % FF_CHUNK == 0
    assert TM_IN == TQ

    row = lambda g: g.reshape(1, -1)
    pos3 = positions.reshape(t // TM_IN, 1, TM_IN)
    inv_col = _rope_inv_col()
    x2 = x.reshape(t, d)
    for l in range(depth):
        mod3 = _mod_call(c, w_ada[l], b_ada[l]).reshape(bsz * 6, 1, d)
        wk_p, wv_p = _relayout_w_kv_b(w_kv_b[l])
        q, k, v, rq, rk, rv, rg = _inproj_call(
            x2, mod3, pos3, inv_col, row(pre_norm_mix[l]), _relayout_w_in(w_in[l]),
            row(q_a_norm[l]), _relayout_w_q_b(w_q_b[l]), row(kv_a_norm[l]), wk_p, wv_p, seq)
        b3 = lambda a: a.reshape(bsz, seq, a.shape[-1])
        y_mla = _attn_call(b3(q), b3(k), v)
        y_ret = _ret_call(b3(rq), b3(rk), b3(rv), b3(rg), row(ret_gn_gain[l]))
        n_ch = d_ff // FF_CHUNK
        wg3 = _bf16(w_gate[l]).reshape(d, n_ch, FF_CHUNK).transpose(1, 0, 2)
        wu3 = _bf16(w_up[l]).reshape(d, n_ch, FF_CHUNK).transpose(1, 0, 2)
        x2 = _outffn_call(
            x2, y_mla.reshape(t, MLA_WIDTH), y_ret.reshape(t, RET_WIDTH), mod3,
            row(mla_out_norm[l]), row(post_norm_mix[l]), row(pre_norm_ffn[l]), row(post_norm_ffn[l]),
            _bf16(w_out[l]), wg3, wu3, _bf16(w_down[l]), seq)
    return x2.reshape(bsz, seq, d)
```

```python
import functools

import jax
import jax.numpy as jnp
import numpy as np
from jax import lax
from jax.experimental import pallas as pl
from jax.experimental.pallas import tpu as pltpu

MLA_HEADS = 8
MLA_NOPE = 64
MLA_ROPE = 32
MLA_V = 64
MLA_Q_RANK = 384
MLA_KV_RANK = 256
MLA_QK = MLA_NOPE + MLA_ROPE
RET_HEADS = 4
RET_QK = 64
RET_V = 128
MLA_WIDTH = MLA_HEADS * MLA_V
RET_WIDTH = RET_HEADS * RET_V
ROPE_BASE = 10000.0
EPS = 1e-6
LOG2E = 1.4426950408889634

LANES = 128
VMEM_LIMIT = 56 * 1024 * 1024

TM_IN = 512
TQ = 512
RET_C = 256
RET_T = 512
TM_OUT = 512
FF_CHUNK = 256

HEAD_LANES = LANES
ROPE_HALF = MLA_ROPE // 2
RET_HALF = RET_QK // 2
N_FREQ = ROPE_HALF + RET_HALF


def _bf16(x):
    return x.astype(jnp.bfloat16)


def _dot(a, b):
    return jnp.dot(a, b, preferred_element_type=jnp.float32)


def _dot_nt(a, b):
    return lax.dot_general(a, b, (((1,), (1,)), ((), ())), preferred_element_type=jnp.float32)


def _dot_tn(a, b):
    return lax.dot_general(a, b, (((0,), (0,)), ((), ())), preferred_element_type=jnp.float32)


def _rms(x, gain):
    return x * lax.rsqrt(jnp.mean(x * x, axis=-1, keepdims=True) + EPS) * gain


def _const_spec(shape):
    nd = len(shape)
    return pl.BlockSpec(shape, lambda *_: (0,) * nd, pipeline_mode=pl.Buffered(1))


def _mod_kernel(c_ref, w_ref, b_ref, o_ref):
    c = c_ref[...]
    a = _bf16(c * jax.nn.sigmoid(c))
    o_ref[...] = _dot(a, _bf16(w_ref[...])) + b_ref[...]


def _mod_call(c, w_ada, b_ada):
    bsz, d = c.shape
    n = w_ada.shape[1]
    tn = 1024
    return pl.pallas_call(
        _mod_kernel,
        grid=(n // tn,),
        in_specs=[pl.BlockSpec((bsz, d), lambda j: (0, 0)),
                  pl.BlockSpec((d, tn), lambda j: (0, j)),
                  pl.BlockSpec((1, tn), lambda j: (0, j))],
        out_specs=pl.BlockSpec((bsz, tn), lambda j: (0, j)),
        out_shape=jax.ShapeDtypeStruct((bsz, n), jnp.float32),
        compiler_params=pltpu.CompilerParams(dimension_semantics=("arbitrary",),
                                             vmem_limit_bytes=VMEM_LIMIT),
        name="adaln_mod",
    )(c, w_ada, b_ada.reshape(1, n))


_O_CQ = 0
_O_CKV = _O_CQ + MLA_Q_RANK
_O_KPE = _O_CKV + MLA_KV_RANK
_O_RQ = _O_KPE + HEAD_LANES
_O_RK = _O_RQ + RET_HEADS * RET_QK
_O_RV = _O_RK + RET_HEADS * RET_QK
_O_RG = _O_RV + RET_WIDTH
_IN_COLS_P = _O_RG + RET_WIDTH


def _swap_rope_halves(x):
    lane = lax.broadcasted_iota(jnp.int32, x.shape, 1)
    from_right = pltpu.roll(x, HEAD_LANES - ROPE_HALF, 1)
    from_left = pltpu.roll(x, ROPE_HALF, 1)
    return jnp.where(lane < MLA_NOPE + ROPE_HALF, from_right, from_left)


def _inproj_kernel(x_ref, sc_ref, sh_ref, pos_ref, inv_ref, gpre_ref, win_ref, gq_ref, wqb_ref,
                   gkv_ref, wk_ref, wvt_ref,
                   q_ref, k_ref, vt_ref, rq_ref, rk_ref, rv_ref, rg_ref):
    tm = x_ref.shape[0]
    x = x_ref[...]
    h = _bf16(_rms(x, gpre_ref[...]) * (1.0 + sc_ref[0]) + sh_ref[0])

    pos = pos_ref[0].astype(jnp.float32)
    ang = inv_ref[...] * pos
    cs = jnp.cos(ang)
    sn = jnp.sin(ang)
    c_m, c_r = cs[:ROPE_HALF], cs[ROPE_HALF:]
    s_m, s_r = sn[:ROPE_HALF], sn[ROPE_HALF:]
    ones = jnp.ones((MLA_NOPE, tm), jnp.float32)
    zeros_hi = jnp.zeros((HEAD_LANES - MLA_QK, tm), jnp.float32)
    zeros_lo = jnp.zeros((MLA_NOPE, tm), jnp.float32)
    cos_k = jnp.concatenate([ones, c_m, c_m, zeros_hi], axis=0).T
    sin_k = jnp.concatenate([zeros_lo, -s_m, s_m, zeros_hi], axis=0).T
    cos_r = jnp.concatenate([c_r] * RET_HEADS, axis=0).T
    sin_r = jnp.concatenate([s_r] * RET_HEADS, axis=0).T
    scale = MLA_QK ** -0.5 * LOG2E
    cos_q = cos_k * scale
    sin_q = sin_k * scale

    cq = _dot(h, win_ref[:, _O_CQ:_O_CQ + MLA_Q_RANK])
    q = _dot(_bf16(_rms(cq, gq_ref[...])), wqb_ref[...])
    for hd in range(MLA_HEADS):
        sl = slice(hd * HEAD_LANES, (hd + 1) * HEAD_LANES)
        qh = q[:, sl]
        q_ref[:, sl] = _bf16(qh * cos_q + _swap_rope_halves(qh) * sin_q)

    ckv = _dot(h, win_ref[:, _O_CKV:_O_CKV + MLA_KV_RANK])
    ckvn = _bf16(_rms(ckv, gkv_ref[...]))
    kpe = _dot(h, win_ref[:, _O_KPE:_O_KPE + HEAD_LANES])
    kpe = kpe * cos_k + _swap_rope_halves(kpe) * sin_k
    kn = _dot(ckvn, wk_ref[...])
    for hd in range(MLA_HEADS):
        sl = slice(hd * HEAD_LANES, (hd + 1) * HEAD_LANES)
        k_ref[:, sl] = _bf16(kn[:, sl] + kpe)
    vt_ref[0, 0] = _bf16(_dot_nt(wvt_ref[...], ckvn))

    for off, o_ref in ((_O_RQ, rq_ref), (_O_RK, rk_ref)):
        z = _dot(h, win_ref[:, off:off + 2 * LANES])
        z1, z2 = z[:, :LANES], z[:, LANES:]
        o_ref[:, :LANES] = _bf16(z1 * cos_r - z2 * sin_r)
        o_ref[:, LANES:] = _bf16(z1 * sin_r + z2 * cos_r)
    rv_ref[...] = _bf16(_dot(h, win_ref[:, _O_RV:_O_RV + RET_WIDTH]))
    rg_ref[...] = _bf16(_dot(h, win_ref[:, _O_RG:_O_RG + RET_WIDTH]))


def _inproj_call(x2, mod3, pos3, inv_col, gpre, win_p, gq, wqb_p, gkv, wk_p, wv_p, seq):
    t, d = x2.shape
    tm = TM_IN
    per_b = seq // tm
    tok = lambda n: pl.BlockSpec((tm, n), lambda i: (i, 0))
    mod_spec = lambda k: pl.BlockSpec((1, 1, d), lambda i: ((i // per_b) * 6 + k, 0, 0))
    outs = [(MLA_HEADS * HEAD_LANES), (MLA_HEADS * HEAD_LANES), None,
            RET_HEADS * RET_QK, RET_HEADS * RET_QK, RET_WIDTH, RET_WIDTH]
    vt_spec = pl.BlockSpec((1, 1, MLA_WIDTH, tm), lambda i: (i // per_b, i % per_b, 0, 0))
    vt_shape = jax.ShapeDtypeStruct((t // seq, per_b, MLA_WIDTH, tm), jnp.bfloat16)
    return pl.pallas_call(
        _inproj_kernel,
        grid=(t // tm,),
        in_specs=[tok(d), mod_spec(1), mod_spec(0),
                  pl.BlockSpec((1, 1, tm), lambda i: (i, 0, 0)),
                  _const_spec(inv_col.shape), _const_spec(gpre.shape), _const_spec(win_p.shape),
                  _const_spec(gq.shape), _const_spec(wqb_p.shape), _const_spec(gkv.shape),
                  _const_spec(wk_p.shape), _const_spec(wv_p.shape)],
        out_specs=[vt_spec if n is None else tok(n) for n in outs],
        out_shape=[vt_shape if n is None else jax.ShapeDtypeStruct((t, n), jnp.bfloat16) for n in outs],
        compiler_params=pltpu.CompilerParams(dimension_semantics=("arbitrary",),
                                             vmem_limit_bytes=VMEM_LIMIT),
        name="inproj",
    )(x2, mod3, mod3, pos3, inv_col, gpre, win_p, gq, wqb_p, gkv, wk_p, wv_p)


Q_SUB = 256
ONES_ROWS = 16


def _attn_kernel(q_ref, k_ref, vt_ref, bias_ref, o_ref, m_ref, acc_ref, sa_ref, sb_ref, mxa_ref, mxb_ref):
    tq = TQ
    n_q = q_ref.shape[1] // tq
    ones = jnp.ones((ONES_ROWS, tq), jnp.bfloat16)
    subs = [(hd, slice(qh * Q_SUB, (qh + 1) * Q_SUB)) for hd in range(2) for qh in range(tq // Q_SUB)]

    def produce(qi, j, s_ref, mx_ref):
        start = j * tq if isinstance(j, int) else pl.multiple_of(j * tq, tq)
        for i, (hd, qs) in enumerate(subs):
            sl = slice(hd * HEAD_LANES, (hd + 1) * HEAD_LANES)
            q = q_ref[0, qi * tq + qs.start:qi * tq + qs.stop, sl]
            st = _dot_nt(k_ref[0, pl.ds(start, tq), sl], q)
            s_ref[i] = st
            mx_ref[i] = jnp.max(st, axis=0, keepdims=True)

    def consume(j, s_ref, mx_ref, masked):
        for i, (hd, qs) in enumerate(subs):
            vt = vt_ref[0, j, hd * MLA_V:(hd + 1) * MLA_V, :]
            lhs = jnp.concatenate([vt, ones], axis=0)
            if masked:
                st = s_ref[i] + bias_ref[:, qs]
                m_cur = jnp.max(st, axis=0, keepdims=True)
            else:
                st = s_ref[i]
                m_cur = mx_ref[i]
            m_prev = m_ref[hd, :, qs]
            m_new = jnp.maximum(m_prev, m_cur)
            alpha = jnp.exp2(m_prev - m_new)
            p = _bf16(jnp.exp2(st - m_new))
            acc_ref[hd, :, qs] = alpha * acc_ref[hd, :, qs] + _dot(lhs, p)
            m_ref[hd, :, qs] = m_new

    def finalize(qi):
        outs = []
        for hd in range(2):
            acc = acc_ref[hd]
            outs.append(acc[:MLA_V] / acc[MLA_V:MLA_V + 1])
        o_ref[0, qi * tq:(qi + 1) * tq, :] = _bf16(jnp.concatenate(outs, axis=0).T)

    buf_a, buf_b = (sa_ref, mxa_ref), (sb_ref, mxb_ref)
    produce(0, 0, *buf_a)
    for qi in range(n_q):
        m_ref[...] = jnp.full(m_ref.shape, -jnp.inf, jnp.float32)
        acc_ref[...] = jnp.zeros(acc_ref.shape, jnp.float32)

        def pair_body(i, carry, qi=qi, buf_a=buf_a, buf_b=buf_b):
            j = 2 * i
            produce(qi, j + 1, *buf_b)
            consume(j, *buf_a, False)
            produce(qi, j + 2, *buf_a)
            consume(j + 1, *buf_b, False)
            return carry

        if qi // 2 > 0:
            lax.fori_loop(0, qi // 2, pair_body, 0)
        if qi % 2 == 1:
            produce(qi, qi, *buf_b)
            consume(qi - 1, *buf_a, False)
            buf_a, buf_b = buf_b, buf_a
        if qi + 1 < n_q:
            produce(qi + 1, 0, *buf_b)
        consume(qi, *buf_a, True)
        finalize(qi)
        buf_a, buf_b = buf_b, buf_a


def _attn_call(q, k, vt):
    bsz, seq, _ = q.shape
    tq = TQ
    pairs = MLA_HEADS // 2
    n_sub = 2 * (tq // Q_SUB)
    idx = jnp.arange(tq)
    bias = jnp.where(idx[:, None] <= idx[None, :], 0.0, -jnp.inf).astype(jnp.float32)
    return pl.pallas_call(
        _attn_kernel,
        grid=(bsz, pairs),
        in_specs=[pl.BlockSpec((1, seq, 2 * HEAD_LANES), lambda b, p: (b, 0, p)),
                  pl.BlockSpec((1, seq, 2 * HEAD_LANES), lambda b, p: (b, 0, p)),
                  pl.BlockSpec((1, seq // tq, 2 * MLA_V, tq), lambda b, p: (b, 0, p, 0)),
                  _const_spec(bias.shape)],
        out_specs=pl.BlockSpec((1, seq, LANES), lambda b, p: (b, 0, p)),
        out_shape=jax.ShapeDtypeStruct((bsz, seq, MLA_WIDTH), jnp.bfloat16),
        scratch_shapes=[pltpu.VMEM((2, 1, tq), jnp.float32),
                        pltpu.VMEM((2, MLA_V + ONES_ROWS, tq), jnp.float32)]
        + [pltpu.VMEM((n_sub, tq, Q_SUB), jnp.float32)] * 2
        + [pltpu.VMEM((n_sub, 1, Q_SUB), jnp.float32)] * 2,
        compiler_params=pltpu.CompilerParams(
            dimension_semantics=("arbitrary", "arbitrary"),
            vmem_limit_bytes=VMEM_LIMIT),
        name="mla_attn",
    )(q, k, vt, bias)


def _ret_kernel(q_ref, k_ref, v_ref, g_ref, dec_ref, wq_ref, wk_ref, cd_ref, gain_ref,
                o_ref, state_ref):
    @pl.when(pl.program_id(1) == 0)
    def _():
        state_ref[...] = jnp.zeros(state_ref.shape, jnp.float32)

    c = RET_C
    lane = lax.broadcasted_iota(jnp.int32, (1, 2 * LANES), 1)
    for ci in range(q_ref.shape[1] // c):
        rows = slice(ci * c, (ci + 1) * c)
        q = q_ref[0, rows, :].astype(jnp.float32)
        k = k_ref[0, rows, :]
        kf = k.astype(jnp.float32)
        for hd in range(RET_HEADS):
            vh = v_ref[0, rows, hd * RET_V:(hd + 1) * RET_V]
            head = ((lane % LANES) // RET_HALF) == hd
            qm = jnp.where(head, q, 0.0)
            sc = _dot_nt(_bf16(qm), k) * dec_ref[hd]
            inner = _dot(_bf16(sc), vh)
            st = state_ref[hd]
            cross = _dot(_bf16(qm * wq_ref[hd]), _bf16(st))
            u = _dot_tn(_bf16(kf * wk_ref[hd]), vh)
            state_ref[hd] = st * cd_ref[hd] + u
            o = inner + cross
            mu = jnp.mean(o, axis=-1, keepdims=True)
            var = jnp.mean(jnp.square(o - mu), axis=-1, keepdims=True)
            cols = slice(hd * RET_V, (hd + 1) * RET_V)
            on = (o - mu) * lax.rsqrt(var + EPS) * gain_ref[:, cols]
            g = g_ref[0, rows, cols].astype(jnp.float32)
            o_ref[0, rows, cols] = _bf16(g * jax.nn.sigmoid(g) * on)


def _ret_tables():
    c = RET_C
    f32 = jnp.float32
    log_gamma = jnp.log(1.0 - 2.0 ** (-5.0 - jnp.arange(RET_HEADS, dtype=f32)))
    idx = jnp.arange(c)
    rel = idx[:, None] - idx[None, :]
    k_scale = RET_QK ** -0.5
    dec = jnp.where(rel >= 0, jnp.exp(log_gamma[:, None, None] * jnp.maximum(rel, 0).astype(f32)), 0.0)
    dec = dec * k_scale
    w_q = jnp.exp(log_gamma[:, None] * (idx + 1).astype(f32))[:, :, None]
    w_k = (jnp.exp(log_gamma[:, None] * (c - 1 - idx).astype(f32)) * k_scale)[:, :, None]
    cd = jnp.exp(log_gamma * c)[:, None, None]
    w_q = jnp.broadcast_to(w_q, (RET_HEADS, c, 2 * LANES))
    w_k = jnp.broadcast_to(w_k, (RET_HEADS, c, 2 * LANES))
    cd = jnp.broadcast_to(cd, (RET_HEADS, 1, LANES))
    return dec, w_q, w_k, cd


def _ret_call(rq, rk, rv, rg, gn_gain):
    bsz, seq, _ = rq.shape
    tt = RET_T
    dec, w_q, w_k, cd = _ret_tables()
    tok = lambda n: pl.BlockSpec((1, tt, n), lambda b, i: (b, i, 0))
    return pl.pallas_call(
        _ret_kernel,
        grid=(bsz, seq // tt),
        in_specs=[tok(2 * LANES), tok(2 * LANES), tok(RET_WIDTH), tok(RET_WIDTH),
                  _const_spec(dec.shape), _const_spec(w_q.shape), _const_spec(w_k.shape),
                  _const_spec(cd.shape), _const_spec(gn_gain.shape)],
        out_specs=tok(RET_WIDTH),
        out_shape=jax.ShapeDtypeStruct((bsz, seq, RET_WIDTH), jnp.bfloat16),
        scratch_shapes=[pltpu.VMEM((RET_HEADS, 2 * LANES, RET_V), jnp.float32)],
        compiler_params=pltpu.CompilerParams(dimension_semantics=("arbitrary", "arbitrary"),
                                             vmem_limit_bytes=VMEM_LIMIT),
        name="retention",
    )(rq, rk, rv, rg, dec, w_q, w_k, cd, gn_gain)


def _outffn_kernel(x_ref, ym_ref, yr_ref, g1_ref, sh2_ref, sc2_ref, g2_ref,
                   gmla_ref, gpost_ref, gpre2_ref, gpost2_ref,
                   wout_ref, wg_ref, wu_ref, wd_ref, o_ref, h_ref, a_ref):
    ymn = _bf16(_rms(ym_ref[...].astype(jnp.float32), gmla_ref[...]))
    mix = _dot(ymn, wout_ref[:MLA_WIDTH, :]) + _dot(yr_ref[...], wout_ref[MLA_WIDTH:, :])
    x1 = x_ref[...] + g1_ref[0] * _rms(mix, gpost_ref[...])
    o_ref[...] = x1
    h_ref[...] = _bf16(_rms(x1, gpre2_ref[...]) * (1.0 + sc2_ref[0]) + sh2_ref[0])

    n_chunks = wg_ref.shape[0]
    for ci in range(n_chunks):
        h = h_ref[...]
        g = _dot(h, wg_ref[ci])
        u = _dot(h, wu_ref[ci])
        a_ref[:, ci * FF_CHUNK:(ci + 1) * FF_CHUNK] = _bf16(g * jax.nn.sigmoid(g) * u)
    f = _dot(a_ref[...], wd_ref[...])
    o_ref[...] = o_ref[...] + g2_ref[0] * _rms(f, gpost2_ref[...])


def _outffn_call(x2, ym, yr, mod3, gmla, gpost, gpre2, gpost2, wout, wg3, wu3, wd, seq):
    t, d = x2.shape
    tm = TM_OUT
    per_b = seq // tm
    d_ff = wd.shape[0]
    tok = lambda n: pl.BlockSpec((tm, n), lambda i: (i, 0))
    mod_spec = lambda k: pl.BlockSpec((1, 1, d), lambda i: ((i // per_b) * 6 + k, 0, 0))
    return pl.pallas_call(
        _outffn_kernel,
        grid=(t // tm,),
        in_specs=[tok(d), tok(MLA_WIDTH), tok(RET_WIDTH),
                  mod_spec(2), mod_spec(3), mod_spec(4), mod_spec(5),
                  _const_spec(gmla.shape), _const_spec(gpost.shape), _const_spec(gpre2.shape),
                  _const_spec(gpost2.shape), _const_spec(wout.shape), _const_spec(wg3.shape),
                  _const_spec(wu3.shape), _const_spec(wd.shape)],
        out_specs=tok(d),
        out_shape=jax.ShapeDtypeStruct((t, d), jnp.float32),
        scratch_shapes=[pltpu.VMEM((tm, d), jnp.bfloat16), pltpu.VMEM((tm, d_ff), jnp.bfloat16)],
        compiler_params=pltpu.CompilerParams(dimension_semantics=("arbitrary",),
                                             vmem_limit_bytes=VMEM_LIMIT),
        name="outproj_ffn",
    )(x2, ym, yr, mod3, mod3, mod3, mod3, gmla, gpost, gpre2, gpost2, wout, wg3, wu3, wd)


def _relayout_w_in(w_in):
    d = w_in.shape[0]
    o = np.cumsum([0, MLA_Q_RANK, MLA_KV_RANK, MLA_ROPE, RET_HEADS * RET_QK, RET_HEADS * RET_QK,
                   RET_WIDTH, RET_WIDTH])
    cq, ckv, kpe, rq, rk, rv, rg = [w_in[:, o[i]:o[i + 1]] for i in range(7)]
    kpe = jnp.concatenate([jnp.zeros((d, MLA_NOPE), w_in.dtype), kpe,
                           jnp.zeros((d, HEAD_LANES - MLA_QK), w_in.dtype)], axis=1)

    def halves_first(w):
        return w.reshape(d, RET_HEADS, 2, RET_HALF).transpose(0, 2, 1, 3).reshape(d, RET_HEADS * RET_QK)

    return _bf16(jnp.concatenate([cq, ckv, kpe, halves_first(rq), halves_first(rk), rv, rg], axis=1))


def _relayout_w_q_b(w):
    r = w.shape[0]
    w = w.reshape(r, MLA_HEADS, MLA_QK)
    w = jnp.pad(w, ((0, 0), (0, 0), (0, HEAD_LANES - MLA_QK)))
    return _bf16(w.reshape(r, MLA_HEADS * HEAD_LANES))


def _relayout_w_kv_b(w):
    r = w.shape[0]
    w = w.reshape(r, MLA_HEADS, MLA_NOPE + MLA_V)
    wk = jnp.pad(w[..., :MLA_NOPE], ((0, 0), (0, 0), (0, HEAD_LANES - MLA_NOPE)))
    wv = w[..., MLA_NOPE:]
    return _bf16(wk.reshape(r, MLA_HEADS * HEAD_LANES)), _bf16(wv.reshape(r, MLA_WIDTH).T)


def _rope_inv_col():
    inv_m = ROPE_BASE ** (-jnp.arange(0, MLA_ROPE, 2, dtype=jnp.float32) / MLA_ROPE)
    inv_r = ROPE_BASE ** (-jnp.arange(0, RET_QK, 2, dtype=jnp.float32) / RET_QK)
    return jnp.concatenate([inv_m, inv_r]).reshape(N_FREQ, 1)


def kernel(x, c, positions, w_ada, b_ada, pre_norm_mix, w_in, q_a_norm, w_q_b, kv_a_norm, w_kv_b,
           mla_out_norm, ret_gn_gain, w_out, post_norm_mix, pre_norm_ffn, w_gate, w_up, w_down,
           post_norm_ffn):
    bsz, seq, d = x.shape
    t = bsz * seq
    depth = w_ada.shape[0]
    d_ff = w_gate.shape[-1]
    assert seq % TM_IN == 0 and seq % TQ == 0 and seq % RET_T == 0 and seq % TM_OUT == 0
    assert RET_T % RET_C == 0 and d_ff % FF_CHUNK == 0
    assert TM_IN == TQ

    row = lambda g: g.reshape(1, -1)
    pos3 = positions.reshape(t // TM_IN, 1, TM_IN)
    inv_col = _rope_inv_col()
    x2 = x.reshape(t, d)
    for l in range(depth):
        mod3 = _mod_call(c, w_ada[l], b_ada[l]).reshape(bsz * 6, 1, d)
        wk_p, wv_p = _relayout_w_kv_b(w_kv_b[l])
        q, k, v, rq, rk, rv, rg = _inproj_call(
            x2, mod3, pos3, inv_col, row(pre_norm_mix[l]), _relayout_w_in(w_in[l]),
            row(q_a_norm[l]), _relayout_w_q_b(w_q_b[l]), row(kv_a_norm[l]), wk_p, wv_p, seq)
        b3 = lambda a: a.reshape(bsz, seq, a.shape[-1])
        y_mla = _attn_call(b3(q), b3(k), v)
        y_ret = _ret_call(b3(rq), b3(rk), b3(rv), b3(rg), row(ret_gn_gain[l]))
        n_ch = d_ff // FF_CHUNK
        wg3 = _bf16(w_gate[l]).reshape(d, n_ch, FF_CHUNK).transpose(1, 0, 2)
        wu3 = _bf16(w_up[l]).reshape(d, n_ch, FF_CHUNK).transpose(1, 0, 2)
        x2 = _outffn_call(
            x2, y_mla.reshape(t, MLA_WIDTH), y_ret.reshape(t, RET_WIDTH), mod3,
            row(mla_out_norm[l]), row(post_norm_mix[l]), row(pre_norm_ffn[l]), row(post_norm_ffn[l]),
            _bf16(w_out[l]), wg3, wu3, _bf16(w_down[l]), seq)
    return x2.reshape(bsz, seq, d)
```

```python
import functools

import jax
import jax.numpy as jnp
import numpy as np
from jax import lax
from jax.experimental import pallas as pl
from jax.experimental.pallas import tpu as pltpu

MLA_HEADS = 8
MLA_NOPE = 64
MLA_ROPE = 32
MLA_V = 64
MLA_Q_RANK = 384
MLA_KV_RANK = 256
MLA_QK = MLA_NOPE + MLA_ROPE
RET_HEADS = 4
RET_QK = 64
RET_V = 128
MLA_WIDTH = MLA_HEADS * MLA_V
RET_WIDTH = RET_HEADS * RET_V
ROPE_BASE = 10000.0
EPS = 1e-6
LOG2E = 1.4426950408889634

LANES = 128
VMEM_LIMIT = 56 * 1024 * 1024

TM_IN = 512
IN_ROWS = 512
TQ = 512
RET_C = 256
RET_T = 512
TM_OUT = 1024
FF_CHUNK = 256
FF_ROWS = 512

HEAD_LANES = LANES
ROPE_HALF = MLA_ROPE // 2
RET_HALF = RET_QK // 2
N_FREQ = ROPE_HALF + RET_HALF


def _bf16(x):
    return x.astype(jnp.bfloat16)


def _dot(a, b):
    return jnp.dot(a, b, preferred_element_type=jnp.float32)


def _dot_nt(a, b):
    return lax.dot_general(a, b, (((1,), (1,)), ((), ())), preferred_element_type=jnp.float32)


def _dot_tn(a, b):
    return lax.dot_general(a, b, (((0,), (0,)), ((), ())), preferred_element_type=jnp.float32)


def _rms(x, gain):
    return x * lax.rsqrt(jnp.mean(x * x, axis=-1, keepdims=True) + EPS) * gain


def _const_spec(shape):
    nd = len(shape)
    return pl.BlockSpec(shape, lambda *_: (0,) * nd, pipeline_mode=pl.Buffered(1))


def _mod_kernel(c_ref, w_ref, b_ref, o_ref):
    c = c_ref[...]
    a = _bf16(c * jax.nn.sigmoid(c))
    o_ref[...] = _dot(a, _bf16(w_ref[...])) + b_ref[...]


def _mod_call(c, w_ada, b_ada):
    bsz, d = c.shape
    n = w_ada.shape[1]
    tn = 1024
    return pl.pallas_call(
        _mod_kernel,
        grid=(n // tn,),
        in_specs=[pl.BlockSpec((bsz, d), lambda j: (0, 0)),
                  pl.BlockSpec((d, tn), lambda j: (0, j)),
                  pl.BlockSpec((1, tn), lambda j: (0, j))],
        out_specs=pl.BlockSpec((bsz, tn), lambda j: (0, j)),
        out_shape=jax.ShapeDtypeStruct((bsz, n), jnp.float32),
        compiler_params=pltpu.CompilerParams(dimension_semantics=("arbitrary",),
                                             vmem_limit_bytes=VMEM_LIMIT),
        name="adaln_mod",
    )(c, w_ada, b_ada.reshape(1, n))


_O_CQ = 0
_O_CKV = _O_CQ + MLA_Q_RANK
_O_KPE = _O_CKV + MLA_KV_RANK
_O_RQ = _O_KPE + HEAD_LANES
_O_RK = _O_RQ + RET_HEADS * RET_QK
_O_RV = _O_RK + RET_HEADS * RET_QK
_O_RG = _O_RV + RET_WIDTH
_IN_COLS_P = _O_RG + RET_WIDTH


def _swap_rope_halves(x):
    lane = lax.broadcasted_iota(jnp.int32, x.shape, 1)
    from_right = pltpu.roll(x, HEAD_LANES - ROPE_HALF, 1)
    from_left = pltpu.roll(x, ROPE_HALF, 1)
    return jnp.where(lane < MLA_NOPE + ROPE_HALF, from_right, from_left)


def _inproj_kernel(x_ref, sc_ref, sh_ref, pos_ref, inv_ref, gpre_ref, win_ref, gq_ref, wqb_ref,
                   gkv_ref, wk_ref, wvt_ref,
                   q_ref, k_ref, vt_ref, rq_ref, rk_ref, rv_ref, rg_ref):
    for g in range(x_ref.shape[0] // IN_ROWS):
        _inproj_rows(g, x_ref, sc_ref, sh_ref, pos_ref, inv_ref, gpre_ref, win_ref, gq_ref, wqb_ref,
                     gkv_ref, wk_ref, wvt_ref, q_ref, k_ref, vt_ref, rq_ref, rk_ref, rv_ref, rg_ref)


def _inproj_rows(g, x_ref, sc_ref, sh_ref, pos_ref, inv_ref, gpre_ref, win_ref, gq_ref, wqb_ref,
                 gkv_ref, wk_ref, wvt_ref, q_ref, k_ref, vt_ref, rq_ref, rk_ref, rv_ref, rg_ref):
    tm = IN_ROWS
    rows = slice(g * tm, (g + 1) * tm)
    x = x_ref[rows, :]
    h = _bf16(_rms(x, gpre_ref[...]) * (1.0 + sc_ref[0]) + sh_ref[0])

    pos = pos_ref[0, :, rows].astype(jnp.float32)
    ang = inv_ref[...] * pos
    cs = jnp.cos(ang)
    sn = jnp.sin(ang)
    c_m, c_r = cs[:ROPE_HALF], cs[ROPE_HALF:]
    s_m, s_r = sn[:ROPE_HALF], sn[ROPE_HALF:]
    ones = jnp.ones((MLA_NOPE, tm), jnp.float32)
    zeros_hi = jnp.zeros((HEAD_LANES - MLA_QK, tm), jnp.float32)
    zeros_lo = jnp.zeros((MLA_NOPE, tm), jnp.float32)
    cos_k = jnp.concatenate([ones, c_m, c_m, zeros_hi], axis=0).T
    sin_k = jnp.concatenate([zeros_lo, -s_m, s_m, zeros_hi], axis=0).T
    cos_r = jnp.concatenate([c_r] * RET_HEADS, axis=0).T
    sin_r = jnp.concatenate([s_r] * RET_HEADS, axis=0).T
    scale = MLA_QK ** -0.5 * LOG2E
    cos_q = cos_k * scale
    sin_q = sin_k * scale

    cq = _dot(h, win_ref[:, _O_CQ:_O_CQ + MLA_Q_RANK])
    q = _dot(_bf16(_rms(cq, gq_ref[...])), wqb_ref[...])
    for hd in range(MLA_HEADS):
        sl = slice(hd * HEAD_LANES, (hd + 1) * HEAD_LANES)
        qh = q[:, sl]
        q_ref[rows, sl] = _bf16(qh * cos_q + _swap_rope_halves(qh) * sin_q)

    ckv = _dot(h, win_ref[:, _O_CKV:_O_CKV + MLA_KV_RANK])
    ckvn = _bf16(_rms(ckv, gkv_ref[...]))
    kpe = _dot(h, win_ref[:, _O_KPE:_O_KPE + HEAD_LANES])
    kpe = kpe * cos_k + _swap_rope_halves(kpe) * sin_k
    kn = _dot(ckvn, wk_ref[...])
    for hd in range(MLA_HEADS):
        sl = slice(hd * HEAD_LANES, (hd + 1) * HEAD_LANES)
        k_ref[rows, sl] = _bf16(kn[:, sl] + kpe)
    vt_ref[0, g] = _bf16(_dot_nt(wvt_ref[...], ckvn))

    for off, o_ref in ((_O_RQ, rq_ref), (_O_RK, rk_ref)):
        z = _dot(h, win_ref[:, off:off + 2 * LANES])
        z1, z2 = z[:, :LANES], z[:, LANES:]
        o_ref[rows, :LANES] = _bf16(z1 * cos_r - z2 * sin_r)
        o_ref[rows, LANES:] = _bf16(z1 * sin_r + z2 * cos_r)
    rv_ref[rows, :] = _bf16(_dot(h, win_ref[:, _O_RV:_O_RV + RET_WIDTH]))
    rg_ref[rows, :] = _bf16(_dot(h, win_ref[:, _O_RG:_O_RG + RET_WIDTH]))


def _inproj_call(x2, mod3, pos3, inv_col, gpre, win_p, gq, wqb_p, gkv, wk_p, wv_p, seq):
    t, d = x2.shape
    tm = TM_IN
    per_b = seq // tm
    tok = lambda n: pl.BlockSpec((tm, n), lambda i: (i, 0))
    mod_spec = lambda k: pl.BlockSpec((1, 1, d), lambda i: ((i // per_b) * 6 + k, 0, 0))
    outs = [(MLA_HEADS * HEAD_LANES), (MLA_HEADS * HEAD_LANES), None,
            RET_HEADS * RET_QK, RET_HEADS * RET_QK, RET_WIDTH, RET_WIDTH]
    slabs = tm // IN_ROWS
    vt_spec = pl.BlockSpec((1, slabs, MLA_WIDTH, IN_ROWS), lambda i: (i // per_b, i % per_b, 0, 0))
    vt_shape = jax.ShapeDtypeStruct((t // seq, seq // IN_ROWS, MLA_WIDTH, IN_ROWS), jnp.bfloat16)
    return pl.pallas_call(
        _inproj_kernel,
        grid=(t // tm,),
        in_specs=[tok(d), mod_spec(1), mod_spec(0),
                  pl.BlockSpec((1, 1, tm), lambda i: (i, 0, 0)),
                  _const_spec(inv_col.shape), _const_spec(gpre.shape), _const_spec(win_p.shape),
                  _const_spec(gq.shape), _const_spec(wqb_p.shape), _const_spec(gkv.shape),
                  _const_spec(wk_p.shape), _const_spec(wv_p.shape)],
        out_specs=[vt_spec if n is None else tok(n) for n in outs],
        out_shape=[vt_shape if n is None else jax.ShapeDtypeStruct((t, n), jnp.bfloat16) for n in outs],
        compiler_params=pltpu.CompilerParams(dimension_semantics=("arbitrary",),
                                             vmem_limit_bytes=VMEM_LIMIT),
        name="inproj",
    )(x2, mod3, mod3, pos3, inv_col, gpre, win_p, gq, wqb_p, gkv, wk_p, wv_p)


Q_SUB = 256
ONES_ROWS = 16


def _attn_kernel(q_ref, k_ref, vt_ref, bias_ref, o_ref, m_ref, acc_ref, sa_ref, sb_ref, mxa_ref, mxb_ref):
    tq = TQ
    n_q = q_ref.shape[1] // tq
    ones = jnp.ones((ONES_ROWS, tq), jnp.bfloat16)
    subs = [(hd, slice(qh * Q_SUB, (qh + 1) * Q_SUB)) for hd in range(2) for qh in range(tq // Q_SUB)]

    def produce(qi, j, s_ref, mx_ref):
        start = j * tq if isinstance(j, int) else pl.multiple_of(j * tq, tq)
        for i, (hd, qs) in enumerate(subs):
            sl = slice(hd * HEAD_LANES, (hd + 1) * HEAD_LANES)
            q = q_ref[0, qi * tq + qs.start:qi * tq + qs.stop, sl]
            st = _dot_nt(k_ref[0, pl.ds(start, tq), sl], q)
            s_ref[i] = st
            mx_ref[i] = jnp.max(st, axis=0, keepdims=True)

    def consume(j, s_ref, mx_ref, masked):
        for i, (hd, qs) in enumerate(subs):
            vt = vt_ref[0, j, hd * MLA_V:(hd + 1) * MLA_V, :]
            lhs = jnp.concatenate([vt, ones], axis=0)
            if masked:
                st = s_ref[i] + bias_ref[:, qs]
                m_cur = jnp.max(st, axis=0, keepdims=True)
            else:
                st = s_ref[i]
                m_cur = mx_ref[i]
            m_prev = m_ref[hd, :, qs]
            m_new = jnp.maximum(m_prev, m_cur)
            alpha = jnp.exp2(m_prev - m_new)
            p = _bf16(jnp.exp2(st - m_new))
            acc_ref[hd, :, qs] = alpha * acc_ref[hd, :, qs] + _dot(lhs, p)
            m_ref[hd, :, qs] = m_new

    def finalize(qi):
        outs = []
        for hd in range(2):
            acc = acc_ref[hd]
            outs.append(acc[:MLA_V] / acc[MLA_V:MLA_V + 1])
        o_ref[0, qi * tq:(qi + 1) * tq, :] = _bf16(jnp.concatenate(outs, axis=0).T)

    buf_a, buf_b = (sa_ref, mxa_ref), (sb_ref, mxb_ref)
    produce(0, 0, *buf_a)
    for qi in range(n_q):
        m_ref[...] = jnp.full(m_ref.shape, -jnp.inf, jnp.float32)
        acc_ref[...] = jnp.zeros(acc_ref.shape, jnp.float32)

        def pair_body(i, carry, qi=qi, buf_a=buf_a, buf_b=buf_b):
            j = 2 * i
            produce(qi, j + 1, *buf_b)
            consume(j, *buf_a, False)
            produce(qi, j + 2, *buf_a)
            consume(j + 1, *buf_b, False)
            return carry

        if qi // 2 > 0:
            lax.fori_loop(0, qi // 2, pair_body, 0)
        if qi % 2 == 1:
            produce(qi, qi, *buf_b)
            consume(qi - 1, *buf_a, False)
            buf_a, buf_b = buf_b, buf_a
        if qi + 1 < n_q:
            produce(qi + 1, 0, *buf_b)
        consume(qi, *buf_a, True)
        finalize(qi)
        buf_a, buf_b = buf_b, buf_a


def _attn_call(q, k, vt):
    bsz, seq, _ = q.shape
    tq = TQ
    pairs = MLA_HEADS // 2
    n_sub = 2 * (tq // Q_SUB)
    idx = jnp.arange(tq)
    bias = jnp.where(idx[:, None] <= idx[None, :], 0.0, -jnp.inf).astype(jnp.float32)
    return pl.pallas_call(
        _attn_kernel,
        grid=(bsz, pairs),
        in_specs=[pl.BlockSpec((1, seq, 2 * HEAD_LANES), lambda b, p: (b, 0, p)),
                  pl.BlockSpec((1, seq, 2 * HEAD_LANES), lambda b, p: (b, 0, p)),
                  pl.BlockSpec((1, seq // tq, 2 * MLA_V, tq), lambda b, p: (b, 0, p, 0)),
                  _const_spec(bias.shape)],
        out_specs=pl.BlockSpec((1, seq, LANES), lambda b, p: (b, 0, p)),
        out_shape=jax.ShapeDtypeStruct((bsz, seq, MLA_WIDTH), jnp.bfloat16),
        scratch_shapes=[pltpu.VMEM((2, 1, tq), jnp.float32),
                        pltpu.VMEM((2, MLA_V + ONES_ROWS, tq), jnp.float32)]
        + [pltpu.VMEM((n_sub, tq, Q_SUB), jnp.float32)] * 2
        + [pltpu.VMEM((n_sub, 1, Q_SUB), jnp.float32)] * 2,
        compiler_params=pltpu.CompilerParams(
            dimension_semantics=("arbitrary", "arbitrary"),
            vmem_limit_bytes=VMEM_LIMIT),
        name="mla_attn",
    )(q, k, vt, bias)


def _ret_kernel(q_ref, k_ref, v_ref, g_ref, hmask_ref, dec_ref, wq_ref, wk_ref, cd_ref, gain_ref,
                o_ref, state_ref):
    @pl.when(pl.program_id(1) == 0)
    def _():
        state_ref[...] = jnp.zeros(state_ref.shape, jnp.float32)

    c = RET_C
    for ci in range(q_ref.shape[1] // c):
        rows = slice(ci * c, (ci + 1) * c)
        q = q_ref[0, rows, :]
        k = k_ref[0, rows, :]
        for hd in range(RET_HEADS):
            vh = v_ref[0, rows, hd * RET_V:(hd + 1) * RET_V]
            qm = q * hmask_ref[hd]
            sc = _dot_nt(qm, k) * dec_ref[hd]
            inner = _dot(_bf16(sc), vh)
            st = state_ref[hd]
            cross = _dot(qm, _bf16(st)) * wq_ref[hd]
            u = _dot_tn(k, _bf16(vh.astype(jnp.float32) * wk_ref[hd]))
            state_ref[hd] = st * cd_ref[hd] + u
            o = inner + cross
            mu = jnp.mean(o, axis=-1, keepdims=True)
            var = jnp.mean(jnp.square(o - mu), axis=-1, keepdims=True)
            cols = slice(hd * RET_V, (hd + 1) * RET_V)
            on = (o - mu) * lax.rsqrt(var + EPS) * gain_ref[:, cols]
            g = g_ref[0, rows, cols].astype(jnp.float32)
            o_ref[0, rows, cols] = _bf16(g * jax.nn.sigmoid(g) * on)


def _ret_tables():
    c = RET_C
    f32 = jnp.float32
    log_gamma = jnp.log(1.0 - 2.0 ** (-5.0 - jnp.arange(RET_HEADS, dtype=f32)))
    idx = jnp.arange(c)
    rel = idx[:, None] - idx[None, :]
    k_scale = RET_QK ** -0.5
    dec = jnp.where(rel >= 0, jnp.exp(log_gamma[:, None, None] * jnp.maximum(rel, 0).astype(f32)), 0.0)
    dec = dec * k_scale
    w_q = jnp.exp(log_gamma[:, None] * (idx + 1).astype(f32))[:, :, None]
    w_k = (jnp.exp(log_gamma[:, None] * (c - 1 - idx).astype(f32)) * k_scale)[:, :, None]
    cd = jnp.exp(log_gamma * c)[:, None, None]
    w_q = jnp.broadcast_to(w_q, (RET_HEADS, c, RET_V))
    w_k = jnp.broadcast_to(w_k, (RET_HEADS, c, RET_V))
    cd = jnp.broadcast_to(cd, (RET_HEADS, 1, LANES))
    lane = jnp.arange(2 * LANES)
    hmask = ((lane % LANES) // RET_HALF)[None, :] == jnp.arange(RET_HEADS)[:, None]
    hmask = jnp.broadcast_to(hmask[:, None, :], (RET_HEADS, c, 2 * LANES)).astype(jnp.bfloat16)
    return hmask, dec, w_q, w_k, cd


def _ret_call(rq, rk, rv, rg, gn_gain):
    bsz, seq, _ = rq.shape
    tt = RET_T
    consts = _ret_tables() + (gn_gain,)
    tok = lambda n: pl.BlockSpec((1, tt, n), lambda b, i: (b, i, 0))
    return pl.pallas_call(
        _ret_kernel,
        grid=(bsz, seq // tt),
        in_specs=[tok(2 * LANES), tok(2 * LANES), tok(RET_WIDTH), tok(RET_WIDTH)]
        + [_const_spec(a.shape) for a in consts],
        out_specs=tok(RET_WIDTH),
        out_shape=jax.ShapeDtypeStruct((bsz, seq, RET_WIDTH), jnp.bfloat16),
        scratch_shapes=[pltpu.VMEM((RET_HEADS, 2 * LANES, RET_V), jnp.float32)],
        compiler_params=pltpu.CompilerParams(dimension_semantics=("arbitrary", "arbitrary"),
                                             vmem_limit_bytes=VMEM_LIMIT),
        name="retention",
    )(rq, rk, rv, rg, *consts)


def _outffn_kernel(x_ref, ym_ref, yr_ref, g1_ref, sh2_ref, sc2_ref, g2_ref,
                   gmla_ref, gpost_ref, gpre2_ref, gpost2_ref,
                   wout_ref, wg_ref, wu_ref, wd_ref, o_ref, h_ref, a_ref):
    groups = [slice(r, r + FF_ROWS) for r in range(0, x_ref.shape[0], FF_ROWS)]
    for rows in groups:
        ymn = _bf16(_rms(ym_ref[rows, :].astype(jnp.float32), gmla_ref[...]))
        mix = _dot(ymn, wout_ref[:MLA_WIDTH, :]) + _dot(yr_ref[rows, :], wout_ref[MLA_WIDTH:, :])
        x1 = x_ref[rows, :] + g1_ref[0] * _rms(mix, gpost_ref[...])
        o_ref[rows, :] = x1
        h_ref[rows, :] = _bf16(_rms(x1, gpre2_ref[...]) * (1.0 + sc2_ref[0]) + sh2_ref[0])
    for rows in groups:
        for ci in range(wg_ref.shape[1] // FF_CHUNK):
            cols = slice(ci * FF_CHUNK, (ci + 1) * FF_CHUNK)
            h = h_ref[rows, :]
            g = _dot(h, wg_ref[:, cols])
            u = _dot(h, wu_ref[:, cols])
            a_ref[rows, cols] = _bf16(g * jax.nn.sigmoid(g) * u)
    for rows in groups:
        f = _dot(a_ref[rows, :], wd_ref[...])
        o_ref[rows, :] = o_ref[rows, :] + g2_ref[0] * _rms(f, gpost2_ref[...])


def _outffn_call(x2, ym, yr, mod3, gmla, gpost, gpre2, gpost2, wout, wg3, wu3, wd, seq):
    t, d = x2.shape
    tm = TM_OUT
    per_b = seq // tm
    d_ff = wd.shape[0]
    tok = lambda n: pl.BlockSpec((tm, n), lambda i: (i, 0))
    mod_spec = lambda k: pl.BlockSpec((1, 1, d), lambda i: ((i // per_b) * 6 + k, 0, 0))
    return pl.pallas_call(
        _outffn_kernel,
        grid=(t // tm,),
        in_specs=[tok(d), tok(MLA_WIDTH), tok(RET_WIDTH),
                  mod_spec(2), mod_spec(3), mod_spec(4), mod_spec(5),
                  _const_spec(gmla.shape), _const_spec(gpost.shape), _const_spec(gpre2.shape),
                  _const_spec(gpost2.shape), _const_spec(wout.shape), _const_spec(wg3.shape),
                  _const_spec(wu3.shape), _const_spec(wd.shape)],
        out_specs=tok(d),
        out_shape=jax.ShapeDtypeStruct((t, d), jnp.float32),
        scratch_shapes=[pltpu.VMEM((tm, d), jnp.bfloat16), pltpu.VMEM((tm, d_ff), jnp.bfloat16)],
        compiler_params=pltpu.CompilerParams(dimension_semantics=("arbitrary",),
                                             vmem_limit_bytes=VMEM_LIMIT),
        name="outproj_ffn",
    )(x2, ym, yr, mod3, mod3, mod3, mod3, gmla, gpost, gpre2, gpost2, wout, wg3, wu3, wd)


def _relayout_w_in(w_in):
    d = w_in.shape[0]
    o = np.cumsum([0, MLA_Q_RANK, MLA_KV_RANK, MLA_ROPE, RET_HEADS * RET_QK, RET_HEADS * RET_QK,
                   RET_WIDTH, RET_WIDTH])
    cq, ckv, kpe, rq, rk, rv, rg = [w_in[:, o[i]:o[i + 1]] for i in range(7)]
    kpe = jnp.concatenate([jnp.zeros((d, MLA_NOPE), w_in.dtype), kpe,
                           jnp.zeros((d, HEAD_LANES - MLA_QK), w_in.dtype)], axis=1)

    def halves_first(w):
        return w.reshape(d, RET_HEADS, 2, RET_HALF).transpose(0, 2, 1, 3).reshape(d, RET_HEADS * RET_QK)

    return _bf16(jnp.concatenate([cq, ckv, kpe, halves_first(rq), halves_first(rk), rv, rg], axis=1))


def _relayout_w_q_b(w):
    r = w.shape[0]
    w = w.reshape(r, MLA_HEADS, MLA_QK)
    w = jnp.pad(w, ((0, 0), (0, 0), (0, HEAD_LANES - MLA_QK)))
    return _bf16(w.reshape(r, MLA_HEADS * HEAD_LANES))


def _relayout_w_kv_b(w):
    r = w.shape[0]
    w = w.reshape(r, MLA_HEADS, MLA_NOPE + MLA_V)
    wk = jnp.pad(w[..., :MLA_NOPE], ((0, 0), (0, 0), (0, HEAD_LANES - MLA_NOPE)))
    wv = w[..., MLA_NOPE:]
    return _bf16(wk.reshape(r, MLA_HEADS * HEAD_LANES)), _bf16(wv.reshape(r, MLA_WIDTH).T)


def _rope_inv_col():
    inv_m = ROPE_BASE ** (-jnp.arange(0, MLA_ROPE, 2, dtype=jnp.float32) / MLA_ROPE)
    inv_r = ROPE_BASE ** (-jnp.arange(0, RET_QK, 2, dtype=jnp.float32) / RET_QK)
    return jnp.concatenate([inv_m, inv_r]).reshape(N_FREQ, 1)


def kernel(x, c, positions, w_ada, b_ada, pre_norm_mix, w_in, q_a_norm, w_q_b, kv_a_norm, w_kv_b,
           mla_out_norm, ret_gn_gain, w_out, post_norm_mix, pre_norm_ffn, w_gate, w_up, w_down,
           post_norm_ffn):
    bsz, seq, d = x.shape
    t = bsz * seq
    depth = w_ada.shape[0]
    d_ff = w_gate.shape[-1]
    assert seq % TM_IN == 0 and seq % TQ == 0 and seq % RET_T == 0 and seq % TM_OUT == 0
    assert RET_T % RET_C == 0 and d_ff % FF_CHUNK == 0
    assert IN_ROWS == TQ and TM_IN % IN_ROWS == 0 and TM_OUT % FF_ROWS == 0

    row = lambda g: g.reshape(1, -1)
    pos3 = positions.reshape(t // TM_IN, 1, TM_IN)
    inv_col = _rope_inv_col()
    x2 = x.reshape(t, d)
    for l in range(depth):
        mod3 = _mod_call(c, w_ada[l], b_ada[l]).reshape(bsz * 6, 1, d)
        wk_p, wv_p = _relayout_w_kv_b(w_kv_b[l])
        q, k, v, rq, rk, rv, rg = _inproj_call(
            x2, mod3, pos3, inv_col, row(pre_norm_mix[l]), _relayout_w_in(w_in[l]),
            row(q_a_norm[l]), _relayout_w_q_b(w_q_b[l]), row(kv_a_norm[l]), wk_p, wv_p, seq)
        b3 = lambda a: a.reshape(bsz, seq, a.shape[-1])
        y_mla = _attn_call(b3(q), b3(k), v)
        y_ret = _ret_call(b3(rq), b3(rk), b3(rv), b3(rg), row(ret_gn_gain[l]))
        x2 = _outffn_call(
            x2, y_mla.reshape(t, MLA_WIDTH), y_ret.reshape(t, RET_WIDTH), mod3,
            row(mla_out_norm[l]), row(post_norm_mix[l]), row(pre_norm_ffn[l]), row(post_norm_ffn[l]),
            _bf16(w_out[l]), _bf16(w_gate[l]), _bf16(w_up[l]), _bf16(w_down[l]), seq)
    return x2.reshape(bsz, seq, d)
```

```python
import functools

import jax
import jax.numpy as jnp
import numpy as np
from jax import lax
from jax.experimental import pallas as pl
from jax.experimental.pallas import tpu as pltpu

MLA_HEADS = 8
MLA_NOPE = 64
MLA_ROPE = 32
MLA_V = 64
MLA_Q_RANK = 384
MLA_KV_RANK = 256
MLA_QK = MLA_NOPE + MLA_ROPE
RET_HEADS = 4
RET_QK = 64
RET_V = 128
MLA_WIDTH = MLA_HEADS * MLA_V
RET_WIDTH = RET_HEADS * RET_V
ROPE_BASE = 10000.0
EPS = 1e-6
LOG2E = 1.4426950408889634

LANES = 128
VMEM_LIMIT = 56 * 1024 * 1024

TM_IN = 1024
IN_ROWS = 1024
TQ = 512
AQ = 512
RET_C = 256
RET_T = 512
TM_OUT = 1024
FF_CHUNK = 256
FF_ROWS = 512

HEAD_LANES = LANES
ROPE_HALF = MLA_ROPE // 2
RET_HALF = RET_QK // 2
N_FREQ = ROPE_HALF + RET_HALF


def _bf16(x):
    return x.astype(jnp.bfloat16)


def _dot(a, b):
    return jnp.dot(a, b, preferred_element_type=jnp.float32)


def _dot_nt(a, b):
    return lax.dot_general(a, b, (((1,), (1,)), ((), ())), preferred_element_type=jnp.float32)


def _dot_tn(a, b):
    return lax.dot_general(a, b, (((0,), (0,)), ((), ())), preferred_element_type=jnp.float32)


def _rms(x, gain):
    return x * lax.rsqrt(jnp.mean(x * x, axis=-1, keepdims=True) + EPS) * gain


def _const_spec(shape):
    nd = len(shape)
    return pl.BlockSpec(shape, lambda *_: (0,) * nd, pipeline_mode=pl.Buffered(1))


def _mod_kernel(c_ref, w_ref, b_ref, o_ref):
    c = c_ref[...]
    a = _bf16(c * jax.nn.sigmoid(c))
    o_ref[...] = _dot(a, _bf16(w_ref[...])) + b_ref[...]


def _mod_call(c, w_ada, b_ada):
    bsz, d = c.shape
    n = w_ada.shape[1]
    tn = 1024
    return pl.pallas_call(
        _mod_kernel,
        grid=(n // tn,),
        in_specs=[pl.BlockSpec((bsz, d), lambda j: (0, 0)),
                  pl.BlockSpec((d, tn), lambda j: (0, j)),
                  pl.BlockSpec((1, tn), lambda j: (0, j))],
        out_specs=pl.BlockSpec((bsz, tn), lambda j: (0, j)),
        out_shape=jax.ShapeDtypeStruct((bsz, n), jnp.float32),
        compiler_params=pltpu.CompilerParams(dimension_semantics=("arbitrary",),
                                             vmem_limit_bytes=VMEM_LIMIT),
        name="adaln_mod",
    )(c, w_ada, b_ada.reshape(1, n))


_O_CQ = 0
_O_CKV = _O_CQ + MLA_Q_RANK
_O_KPE = _O_CKV + MLA_KV_RANK
_O_RQ = _O_KPE + HEAD_LANES
_O_RK = _O_RQ + RET_HEADS * RET_QK
_O_RV = _O_RK + RET_HEADS * RET_QK
_O_RG = _O_RV + RET_WIDTH
_IN_COLS_P = _O_RG + RET_WIDTH


def _swap_rope_halves(x):
    lane = lax.broadcasted_iota(jnp.int32, x.shape, 1)
    from_right = pltpu.roll(x, HEAD_LANES - ROPE_HALF, 1)
    from_left = pltpu.roll(x, ROPE_HALF, 1)
    return jnp.where(lane < MLA_NOPE + ROPE_HALF, from_right, from_left)


def _inproj_kernel(x_ref, sc_ref, sh_ref, pos_ref, inv_ref, gpre_ref, win_ref, gq_ref, wqb_ref,
                   gkv_ref, wk_ref, wvt_ref,
                   q_ref, k_ref, vt_ref, rq_ref, rk_ref, rv_ref, rg_ref):
    for g in range(x_ref.shape[0] // IN_ROWS):
        _inproj_rows(g, x_ref, sc_ref, sh_ref, pos_ref, inv_ref, gpre_ref, win_ref, gq_ref, wqb_ref,
                     gkv_ref, wk_ref, wvt_ref, q_ref, k_ref, vt_ref, rq_ref, rk_ref, rv_ref, rg_ref)


def _inproj_rows(g, x_ref, sc_ref, sh_ref, pos_ref, inv_ref, gpre_ref, win_ref, gq_ref, wqb_ref,
                 gkv_ref, wk_ref, wvt_ref, q_ref, k_ref, vt_ref, rq_ref, rk_ref, rv_ref, rg_ref):
    tm = IN_ROWS
    rows = slice(g * tm, (g + 1) * tm)
    x = x_ref[rows, :]
    h = _bf16(_rms(x, gpre_ref[...]) * (1.0 + sc_ref[0]) + sh_ref[0])

    pos = pos_ref[0, :, rows].astype(jnp.float32)
    ang = inv_ref[...] * pos
    cs = jnp.cos(ang)
    sn = jnp.sin(ang)
    c_m, c_r = cs[:ROPE_HALF], cs[ROPE_HALF:]
    s_m, s_r = sn[:ROPE_HALF], sn[ROPE_HALF:]
    ones = jnp.ones((MLA_NOPE, tm), jnp.float32)
    zeros_hi = jnp.zeros((HEAD_LANES - MLA_QK, tm), jnp.float32)
    zeros_lo = jnp.zeros((MLA_NOPE, tm), jnp.float32)
    cos_k = jnp.concatenate([ones, c_m, c_m, zeros_hi], axis=0).T
    sin_k = jnp.concatenate([zeros_lo, -s_m, s_m, zeros_hi], axis=0).T
    cos_r = jnp.concatenate([c_r] * RET_HEADS, axis=0).T
    sin_r = jnp.concatenate([s_r] * RET_HEADS, axis=0).T
    scale = MLA_QK ** -0.5 * LOG2E
    cos_q = cos_k * scale
    sin_q = sin_k * scale

    cq = _dot(h, win_ref[:, _O_CQ:_O_CQ + MLA_Q_RANK])
    q = _dot(_bf16(_rms(cq, gq_ref[...])), wqb_ref[...])
    for hd in range(MLA_HEADS):
        sl = slice(hd * HEAD_LANES, (hd + 1) * HEAD_LANES)
        qh = q[:, sl]
        q_ref[rows, sl] = _bf16(qh * cos_q + _swap_rope_halves(qh) * sin_q)

    ckv = _dot(h, win_ref[:, _O_CKV:_O_CKV + MLA_KV_RANK])
    ckvn = _bf16(_rms(ckv, gkv_ref[...]))
    kpe = _dot(h, win_ref[:, _O_KPE:_O_KPE + HEAD_LANES])
    kpe = kpe * cos_k + _swap_rope_halves(kpe) * sin_k
    kn = _dot(ckvn, wk_ref[...])
    for hd in range(MLA_HEADS):
        sl = slice(hd * HEAD_LANES, (hd + 1) * HEAD_LANES)
        k_ref[rows, sl] = _bf16(kn[:, sl] + kpe)
    vt = _bf16(_dot_nt(wvt_ref[...], ckvn))
    per_group = tm // TQ
    for si in range(per_group):
        vt_ref[0, g * per_group + si] = vt[:, si * TQ:(si + 1) * TQ]

    for off, o_ref in ((_O_RQ, rq_ref), (_O_RK, rk_ref)):
        z = _dot(h, win_ref[:, off:off + 2 * LANES])
        z1, z2 = z[:, :LANES], z[:, LANES:]
        o_ref[rows, :LANES] = _bf16(z1 * cos_r - z2 * sin_r)
        o_ref[rows, LANES:] = _bf16(z1 * sin_r + z2 * cos_r)
    rv_ref[rows, :] = _bf16(_dot(h, win_ref[:, _O_RV:_O_RV + RET_WIDTH]))
    rg_ref[rows, :] = _bf16(_dot(h, win_ref[:, _O_RG:_O_RG + RET_WIDTH]))


def _inproj_call(x2, mod3, pos3, inv_col, gpre, win_p, gq, wqb_p, gkv, wk_p, wv_p, seq):
    t, d = x2.shape
    tm = TM_IN
    per_b = seq // tm
    tok = lambda n: pl.BlockSpec((tm, n), lambda i: (i, 0))
    mod_spec = lambda k: pl.BlockSpec((1, 1, d), lambda i: ((i // per_b) * 6 + k, 0, 0))
    outs = [(MLA_HEADS * HEAD_LANES), (MLA_HEADS * HEAD_LANES), None,
            RET_HEADS * RET_QK, RET_HEADS * RET_QK, RET_WIDTH, RET_WIDTH]
    slabs = tm // TQ
    vt_spec = pl.BlockSpec((1, slabs, MLA_WIDTH, TQ), lambda i: (i // per_b, i % per_b, 0, 0))
    vt_shape = jax.ShapeDtypeStruct((t // seq, seq // TQ, MLA_WIDTH, TQ), jnp.bfloat16)
    return pl.pallas_call(
        _inproj_kernel,
        grid=(t // tm,),
        in_specs=[tok(d), mod_spec(1), mod_spec(0),
                  pl.BlockSpec((1, 1, tm), lambda i: (i, 0, 0)),
                  _const_spec(inv_col.shape), _const_spec(gpre.shape), _const_spec(win_p.shape),
                  _const_spec(gq.shape), _const_spec(wqb_p.shape), _const_spec(gkv.shape),
                  _const_spec(wk_p.shape), _const_spec(wv_p.shape)],
        out_specs=[vt_spec if n is None else tok(n) for n in outs],
        out_shape=[vt_shape if n is None else jax.ShapeDtypeStruct((t, n), jnp.bfloat16) for n in outs],
        compiler_params=pltpu.CompilerParams(dimension_semantics=("arbitrary",),
                                             vmem_limit_bytes=VMEM_LIMIT),
        name="inproj",
    )(x2, mod3, mod3, pos3, inv_col, gpre, win_p, gq, wqb_p, gkv, wk_p, wv_p)


Q_SUB = 256
PV_LAG = 2
ONES_ROWS = 16


def _attn_kernel(q_ref, k_ref, vt_ref, bias_ref, o_ref, m_ref, acc_ref, sa_ref, sb_ref, mxa_ref, mxb_ref):
    tk, tq = TQ, AQ
    r_diag = tq // tk
    n_q = q_ref.shape[1] // tq
    ones = jnp.ones((ONES_ROWS, tk), jnp.bfloat16)
    subs = [(hd, qo) for hd in range(2) for qo in range(0, tq, Q_SUB)]
    full = ["full"] * len(subs)

    def diag_modes(r):
        modes = []
        for _, qo in subs:
            d = qo - r * tk
            modes.append(None if d + Q_SUB <= 0 else ("full" if d >= tk else d))
        return modes

    def produce(qi, j, s_ref, mx_ref, modes, only=None):
        start = j * tk if isinstance(j, int) else pl.multiple_of(j * tk, tk)
        for i, (hd, qo) in enumerate(subs):
            if modes[i] is None or (only is not None and i != only):
                continue
            sl = slice(hd * HEAD_LANES, (hd + 1) * HEAD_LANES)
            q = q_ref[0, qi * tq + qo:qi * tq + qo + Q_SUB, sl]
            st = _dot_nt(k_ref[0, pl.ds(start, tk), sl], q)
            s_ref[i] = st
            mx_ref[i] = jnp.max(st, axis=0, keepdims=True)

    def step(prod, cons):
        for i in range(len(subs) + PV_LAG):
            if prod is not None and i < len(subs):
                produce(*prod, only=i)
            if cons is not None and i >= PV_LAG:
                consume(*cons, only=i - PV_LAG)

    def consume(j, s_ref, mx_ref, modes, only=None):
        for i, (hd, qo) in enumerate(subs):
            if modes[i] is None or (only is not None and i != only):
                continue
            qs = slice(qo, qo + Q_SUB)
            vt = vt_ref[0, j, hd * MLA_V:(hd + 1) * MLA_V, :]
            lhs = jnp.concatenate([vt, ones], axis=0)
            if modes[i] == "full":
                st = s_ref[i]
                m_cur = mx_ref[i]
            else:
                st = s_ref[i] + bias_ref[:, modes[i]:modes[i] + Q_SUB]
                m_cur = jnp.max(st, axis=0, keepdims=True)
            m_prev = m_ref[hd, :, qs]
            m_new = jnp.maximum(m_prev, m_cur)
            alpha = jnp.exp2(m_prev - m_new)
            p = _bf16(jnp.exp2(st - m_new))
            acc_ref[hd, :, qs] = alpha * acc_ref[hd, :, qs] + _dot(lhs, p)
            m_ref[hd, :, qs] = m_new

    def finalize(qi):
        outs = []
        for hd in range(2):
            acc = acc_ref[hd]
            outs.append(acc[:MLA_V] / acc[MLA_V:MLA_V + 1])
        o_ref[0, qi * tq:(qi + 1) * tq, :] = _bf16(jnp.concatenate(outs, axis=0).T)

    buf_a, buf_b = (sa_ref, mxa_ref), (sb_ref, mxb_ref)
    produce(0, 0, *buf_a, full)
    for qi in range(n_q):
        m_ref[...] = jnp.full(m_ref.shape, -jnp.inf, jnp.float32)
        acc_ref[...] = jnp.zeros(acc_ref.shape, jnp.float32)
        n_full = r_diag * qi

        def pair_body(i, carry, qi=qi, buf_a=buf_a, buf_b=buf_b):
            j = 2 * i
            step((qi, j + 1, *buf_b, full), (j, *buf_a, full))
            step((qi, j + 2, *buf_a, full), (j + 1, *buf_b, full))
            return carry

        if n_full // 2 > 0:
            lax.fori_loop(0, n_full // 2, pair_body, 0)
        if n_full % 2 == 1:
            step((qi, n_full, *buf_b, full), (n_full - 1, *buf_a, full))
            buf_a, buf_b = buf_b, buf_a
        for r in range(r_diag):
            nxt = None
            if r + 1 < r_diag:
                nxt = (qi, n_full + r + 1, *buf_b, diag_modes(r + 1))
            elif qi + 1 < n_q:
                nxt = (qi + 1, 0, *buf_b, full)
            step(nxt, (n_full + r, *buf_a, diag_modes(r)))
            buf_a, buf_b = buf_b, buf_a
        finalize(qi)


def _attn_call(q, k, vt):
    bsz, seq, _ = q.shape
    tk, tq = TQ, AQ
    pairs = MLA_HEADS // 2
    n_sub = 2 * (tq // Q_SUB)
    idx = jnp.arange(tk)
    bias = jnp.where(idx[:, None] <= idx[None, :], 0.0, -jnp.inf).astype(jnp.float32)
    return pl.pallas_call(
        _attn_kernel,
        grid=(bsz, pairs),
        in_specs=[pl.BlockSpec((1, seq, 2 * HEAD_LANES), lambda b, p: (b, 0, p)),
                  pl.BlockSpec((1, seq, 2 * HEAD_LANES), lambda b, p: (b, 0, p)),
                  pl.BlockSpec((1, seq // tk, 2 * MLA_V, tk), lambda b, p: (b, 0, p, 0)),
                  _const_spec(bias.shape)],
        out_specs=pl.BlockSpec((1, seq, LANES), lambda b, p: (b, 0, p)),
        out_shape=jax.ShapeDtypeStruct((bsz, seq, MLA_WIDTH), jnp.bfloat16),
        scratch_shapes=[pltpu.VMEM((2, 1, tq), jnp.float32),
                        pltpu.VMEM((2, MLA_V + ONES_ROWS, tq), jnp.float32)]
        + [pltpu.VMEM((n_sub, tk, Q_SUB), jnp.float32)] * 2
        + [pltpu.VMEM((n_sub, 1, Q_SUB), jnp.float32)] * 2,
        compiler_params=pltpu.CompilerParams(
            dimension_semantics=("arbitrary", "arbitrary"),
            vmem_limit_bytes=VMEM_LIMIT),
        name="mla_attn",
    )(q, k, vt, bias)


def _ret_kernel(q_ref, k_ref, v_ref, g_ref, hmask_ref, dec_ref, wq_ref, wk_ref, cd_ref, gain_ref,
                o_ref, state_ref):
    @pl.when(pl.program_id(1) == 0)
    def _():
        state_ref[...] = jnp.zeros(state_ref.shape, jnp.float32)

    c = RET_C
    for ci in range(q_ref.shape[1] // c):
        rows = slice(ci * c, (ci + 1) * c)
        q = q_ref[0, rows, :]
        k = k_ref[0, rows, :]
        for hd in range(RET_HEADS):
            vh = v_ref[0, rows, hd * RET_V:(hd + 1) * RET_V]
            qm = q * hmask_ref[hd]
            sc = _dot_nt(qm, k) * dec_ref[hd]
            inner = _dot(_bf16(sc), vh)
            st = state_ref[hd]
            cross = _dot(qm, _bf16(st)) * wq_ref[hd]
            u = _dot_tn(k, _bf16(vh.astype(jnp.float32) * wk_ref[hd]))
            state_ref[hd] = st * cd_ref[hd] + u
            o = inner + cross
            mu = jnp.mean(o, axis=-1, keepdims=True)
            var = jnp.mean(jnp.square(o - mu), axis=-1, keepdims=True)
            cols = slice(hd * RET_V, (hd + 1) * RET_V)
            on = (o - mu) * lax.rsqrt(var + EPS) * gain_ref[:, cols]
            g = g_ref[0, rows, cols].astype(jnp.float32)
            o_ref[0, rows, cols] = _bf16(g * jax.nn.sigmoid(g) * on)


def _ret_tables():
    c = RET_C
    f32 = jnp.float32
    log_gamma = jnp.log(1.0 - 2.0 ** (-5.0 - jnp.arange(RET_HEADS, dtype=f32)))
    idx = jnp.arange(c)
    rel = idx[:, None] - idx[None, :]
    k_scale = RET_QK ** -0.5
    dec = jnp.where(rel >= 0, jnp.exp(log_gamma[:, None, None] * jnp.maximum(rel, 0).astype(f32)), 0.0)
    dec = dec * k_scale
    w_q = jnp.exp(log_gamma[:, None] * (idx + 1).astype(f32))[:, :, None]
    w_k = (jnp.exp(log_gamma[:, None] * (c - 1 - idx).astype(f32)) * k_scale)[:, :, None]
    cd = jnp.exp(log_gamma * c)[:, None, None]
    w_q = jnp.broadcast_to(w_q, (RET_HEADS, c, RET_V))
    w_k = jnp.broadcast_to(w_k, (RET_HEADS, c, RET_V))
    cd = jnp.broadcast_to(cd, (RET_HEADS, 1, LANES))
    lane = jnp.arange(2 * LANES)
    hmask = ((lane % LANES) // RET_HALF)[None, :] == jnp.arange(RET_HEADS)[:, None]
    hmask = jnp.broadcast_to(hmask[:, None, :], (RET_HEADS, c, 2 * LANES)).astype(jnp.bfloat16)
    return hmask, dec, w_q, w_k, cd


def _ret_call(rq, rk, rv, rg, gn_gain):
    bsz, seq, _ = rq.shape
    tt = RET_T
    consts = _ret_tables() + (gn_gain,)
    tok = lambda n: pl.BlockSpec((1, tt, n), lambda b, i: (b, i, 0))
    return pl.pallas_call(
        _ret_kernel,
        grid=(bsz, seq // tt),
        in_specs=[tok(2 * LANES), tok(2 * LANES), tok(RET_WIDTH), tok(RET_WIDTH)]
        + [_const_spec(a.shape) for a in consts],
        out_specs=tok(RET_WIDTH),
        out_shape=jax.ShapeDtypeStruct((bsz, seq, RET_WIDTH), jnp.bfloat16),
        scratch_shapes=[pltpu.VMEM((RET_HEADS, 2 * LANES, RET_V), jnp.float32)],
        compiler_params=pltpu.CompilerParams(dimension_semantics=("arbitrary", "arbitrary"),
                                             vmem_limit_bytes=VMEM_LIMIT),
        name="retention",
    )(rq, rk, rv, rg, *consts)


def _outffn_kernel(x_ref, ym_ref, yr_ref, g1_ref, sh2_ref, sc2_ref, g2_ref,
                   gmla_ref, gpost_ref, gpre2_ref, gpost2_ref,
                   wout_ref, wg_ref, wu_ref, wd_ref, o_ref, h_ref, a_ref):
    groups = [slice(r, r + FF_ROWS) for r in range(0, x_ref.shape[0], FF_ROWS)]
    for rows in groups:
        ymn = _bf16(_rms(ym_ref[rows, :].astype(jnp.float32), gmla_ref[...]))
        mix = _dot(ymn, wout_ref[:MLA_WIDTH, :]) + _dot(yr_ref[rows, :], wout_ref[MLA_WIDTH:, :])
        x1 = x_ref[rows, :] + g1_ref[0] * _rms(mix, gpost_ref[...])
        o_ref[rows, :] = x1
        h_ref[rows, :] = _bf16(_rms(x1, gpre2_ref[...]) * (1.0 + sc2_ref[0]) + sh2_ref[0])
    for rows in groups:
        for ci in range(wg_ref.shape[1] // FF_CHUNK):
            cols = slice(ci * FF_CHUNK, (ci + 1) * FF_CHUNK)
            h = h_ref[rows, :]
            g = _dot(h, wg_ref[:, cols])
            u = _dot(h, wu_ref[:, cols])
            a_ref[rows, cols] = _bf16(g * jax.nn.sigmoid(g) * u)
    for rows in groups:
        f = _dot(a_ref[rows, :], wd_ref[...])
        o_ref[rows, :] = o_ref[rows, :] + g2_ref[0] * _rms(f, gpost2_ref[...])


def _outffn_call(x2, ym, yr, mod3, gmla, gpost, gpre2, gpost2, wout, wg3, wu3, wd, seq):
    t, d = x2.shape
    tm = TM_OUT
    per_b = seq // tm
    d_ff = wd.shape[0]
    tok = lambda n: pl.BlockSpec((tm, n), lambda i: (i, 0))
    mod_spec = lambda k: pl.BlockSpec((1, 1, d), lambda i: ((i // per_b) * 6 + k, 0, 0))
    return pl.pallas_call(
        _outffn_kernel,
        grid=(t // tm,),
        in_specs=[tok(d), tok(MLA_WIDTH), tok(RET_WIDTH),
                  mod_spec(2), mod_spec(3), mod_spec(4), mod_spec(5),
                  _const_spec(gmla.shape), _const_spec(gpost.shape), _const_spec(gpre2.shape),
                  _const_spec(gpost2.shape), _const_spec(wout.shape), _const_spec(wg3.shape),
                  _const_spec(wu3.shape), _const_spec(wd.shape)],
        out_specs=tok(d),
        out_shape=jax.ShapeDtypeStruct((t, d), jnp.float32),
        scratch_shapes=[pltpu.VMEM((tm, d), jnp.bfloat16), pltpu.VMEM((tm, d_ff), jnp.bfloat16)],
        compiler_params=pltpu.CompilerParams(dimension_semantics=("arbitrary",),
                                             vmem_limit_bytes=VMEM_LIMIT),
        name="outproj_ffn",
    )(x2, ym, yr, mod3, mod3, mod3, mod3, gmla, gpost, gpre2, gpost2, wout, wg3, wu3, wd)


def _relayout_w_in(w_in):
    d = w_in.shape[0]
    o = np.cumsum([0, MLA_Q_RANK, MLA_KV_RANK, MLA_ROPE, RET_HEADS * RET_QK, RET_HEADS * RET_QK,
                   RET_WIDTH, RET_WIDTH])
    cq, ckv, kpe, rq, rk, rv, rg = [w_in[:, o[i]:o[i + 1]] for i in range(7)]
    kpe = jnp.concatenate([jnp.zeros((d, MLA_NOPE), w_in.dtype), kpe,
                           jnp.zeros((d, HEAD_LANES - MLA_QK), w_in.dtype)], axis=1)

    def halves_first(w):
        return w.reshape(d, RET_HEADS, 2, RET_HALF).transpose(0, 2, 1, 3).reshape(d, RET_HEADS * RET_QK)

    return _bf16(jnp.concatenate([cq, ckv, kpe, halves_first(rq), halves_first(rk), rv, rg], axis=1))


def _relayout_w_q_b(w):
    r = w.shape[0]
    w = w.reshape(r, MLA_HEADS, MLA_QK)
    w = jnp.pad(w, ((0, 0), (0, 0), (0, HEAD_LANES - MLA_QK)))
    return _bf16(w.reshape(r, MLA_HEADS * HEAD_LANES))


def _relayout_w_kv_b(w):
    r = w.shape[0]
    w = w.reshape(r, MLA_HEADS, MLA_NOPE + MLA_V)
    wk = jnp.pad(w[..., :MLA_NOPE], ((0, 0), (0, 0), (0, HEAD_LANES - MLA_NOPE)))
    wv = w[..., MLA_NOPE:]
    return _bf16(wk.reshape(r, MLA_HEADS * HEAD_LANES)), _bf16(wv.reshape(r, MLA_WIDTH).T)


def _rope_inv_col():
    inv_m = ROPE_BASE ** (-jnp.arange(0, MLA_ROPE, 2, dtype=jnp.float32) / MLA_ROPE)
    inv_r = ROPE_BASE ** (-jnp.arange(0, RET_QK, 2, dtype=jnp.float32) / RET_QK)
    return jnp.concatenate([inv_m, inv_r]).reshape(N_FREQ, 1)


def kernel(x, c, positions, w_ada, b_ada, pre_norm_mix, w_in, q_a_norm, w_q_b, kv_a_norm, w_kv_b,
           mla_out_norm, ret_gn_gain, w_out, post_norm_mix, pre_norm_ffn, w_gate, w_up, w_down,
           post_norm_ffn):
    bsz, seq, d = x.shape
    t = bsz * seq
    depth = w_ada.shape[0]
    d_ff = w_gate.shape[-1]
    assert seq % TM_IN == 0 and seq % AQ == 0 and seq % RET_T == 0 and seq % TM_OUT == 0
    assert AQ % TQ == 0 and TQ % Q_SUB == 0
    assert RET_T % RET_C == 0 and d_ff % FF_CHUNK == 0
    assert IN_ROWS % TQ == 0 and TM_IN % IN_ROWS == 0 and TM_OUT % FF_ROWS == 0

    row = lambda g: g.reshape(1, -1)
    pos3 = positions.reshape(t // TM_IN, 1, TM_IN)
    inv_col = _rope_inv_col()
    x2 = x.reshape(t, d)
    for l in range(depth):
        mod3 = _mod_call(c, w_ada[l], b_ada[l]).reshape(bsz * 6, 1, d)
        wk_p, wv_p = _relayout_w_kv_b(w_kv_b[l])
        q, k, v, rq, rk, rv, rg = _inproj_call(
            x2, mod3, pos3, inv_col, row(pre_norm_mix[l]), _relayout_w_in(w_in[l]),
            row(q_a_norm[l]), _relayout_w_q_b(w_q_b[l]), row(kv_a_norm[l]), wk_p, wv_p, seq)
        b3 = lambda a: a.reshape(bsz, seq, a.shape[-1])
        y_mla = _attn_call(b3(q), b3(k), v)
        y_ret = _ret_call(b3(rq), b3(rk), b3(rv), b3(rg), row(ret_gn_gain[l]))
        x2 = _outffn_call(
            x2, y_mla.reshape(t, MLA_WIDTH), y_ret.reshape(t, RET_WIDTH), mod3,
            row(mla_out_norm[l]), row(post_norm_mix[l]), row(pre_norm_ffn[l]), row(post_norm_ffn[l]),
            _bf16(w_out[l]), _bf16(w_gate[l]), _bf16(w_up[l]), _bf16(w_down[l]), seq)
    return x2.reshape(bsz, seq, d)
```

```python
import functools

import jax
import jax.numpy as jnp
import numpy as np
from jax import lax
from jax.experimental import pallas as pl
from jax.experimental.pallas import tpu as pltpu

MLA_HEADS = 8
MLA_NOPE = 64
MLA_ROPE = 32
MLA_V = 64
MLA_Q_RANK = 384
MLA_KV_RANK = 256
MLA_QK = MLA_NOPE + MLA_ROPE
RET_HEADS = 4
RET_QK = 64
RET_V = 128
MLA_WIDTH = MLA_HEADS * MLA_V
RET_WIDTH = RET_HEADS * RET_V
ROPE_BASE = 10000.0
EPS = 1e-6
LOG2E = 1.4426950408889634

LANES = 128
VMEM_LIMIT = 56 * 1024 * 1024

TM_IN = 1024
IN_ROWS = 1024
TQ = 512
AQ = 512
RET_C = 256
RET_T = 512
TM_OUT = 1024
FF_CHUNK = 256
FF_ROWS = 512

HEAD_LANES = LANES
ROPE_HALF = MLA_ROPE // 2
RET_HALF = RET_QK // 2
N_FREQ = ROPE_HALF + RET_HALF


def _bf16(x):
    return x.astype(jnp.bfloat16)


def _dot(a, b):
    return jnp.dot(a, b, preferred_element_type=jnp.float32)


def _dot_nt(a, b):
    return lax.dot_general(a, b, (((1,), (1,)), ((), ())), preferred_element_type=jnp.float32)


def _dot_tn(a, b):
    return lax.dot_general(a, b, (((0,), (0,)), ((), ())), preferred_element_type=jnp.float32)


def _rms(x, gain):
    return x * lax.rsqrt(jnp.mean(x * x, axis=-1, keepdims=True) + EPS) * gain


def _const_spec(shape):
    nd = len(shape)
    return pl.BlockSpec(shape, lambda *_: (0,) * nd, pipeline_mode=pl.Buffered(1))


def _mod_kernel(c_ref, w_ref, b_ref, o_ref):
    c = c_ref[...]
    a = _bf16(c * jax.nn.sigmoid(c))
    o_ref[...] = _dot(a, _bf16(w_ref[...])) + b_ref[...]


def _mod_call(c, w_ada, b_ada):
    bsz, d = c.shape
    n = w_ada.shape[1]
    tn = 1024
    return pl.pallas_call(
        _mod_kernel,
        grid=(n // tn,),
        in_specs=[pl.BlockSpec((bsz, d), lambda j: (0, 0)),
                  pl.BlockSpec((d, tn), lambda j: (0, j)),
                  pl.BlockSpec((1, tn), lambda j: (0, j))],
        out_specs=pl.BlockSpec((bsz, tn), lambda j: (0, j)),
        out_shape=jax.ShapeDtypeStruct((bsz, n), jnp.float32),
        compiler_params=pltpu.CompilerParams(dimension_semantics=("arbitrary",),
                                             vmem_limit_bytes=VMEM_LIMIT),
        name="adaln_mod",
    )(c, w_ada, b_ada.reshape(1, n))


_O_CQ = 0
_O_CKV = _O_CQ + MLA_Q_RANK
_O_KPE = _O_CKV + MLA_KV_RANK
_O_RQ = _O_KPE + HEAD_LANES
_O_RK = _O_RQ + RET_HEADS * RET_QK
_O_RV = _O_RK + RET_HEADS * RET_QK
_O_RG = _O_RV + RET_WIDTH
_IN_COLS_P = _O_RG + RET_WIDTH


def _swap_rope_halves(x):
    lane = lax.broadcasted_iota(jnp.int32, x.shape, 1)
    from_right = pltpu.roll(x, HEAD_LANES - ROPE_HALF, 1)
    from_left = pltpu.roll(x, ROPE_HALF, 1)
    return jnp.where(lane < MLA_NOPE + ROPE_HALF, from_right, from_left)


def _inproj_kernel(x_ref, sc_ref, sh_ref, pos_ref, inv_ref, gpre_ref, win_ref, gq_ref, wqbt_ref,
                   gkv_ref, wk_ref, wvt_ref,
                   qt_ref, k_ref, vt_ref, rq_ref, rk_ref, rv_ref, rg_ref):
    for g in range(x_ref.shape[0] // IN_ROWS):
        _inproj_rows(g, x_ref, sc_ref, sh_ref, pos_ref, inv_ref, gpre_ref, win_ref, gq_ref, wqbt_ref,
                     gkv_ref, wk_ref, wvt_ref, qt_ref, k_ref, vt_ref, rq_ref, rk_ref, rv_ref, rg_ref)


def _inproj_rows(g, x_ref, sc_ref, sh_ref, pos_ref, inv_ref, gpre_ref, win_ref, gq_ref, wqbt_ref,
                 gkv_ref, wk_ref, wvt_ref, qt_ref, k_ref, vt_ref, rq_ref, rk_ref, rv_ref, rg_ref):
    tm = IN_ROWS
    rows = slice(g * tm, (g + 1) * tm)
    x = x_ref[rows, :]
    h = _bf16(_rms(x, gpre_ref[...]) * (1.0 + sc_ref[0]) + sh_ref[0])

    pos = pos_ref[0, :, rows].astype(jnp.float32)
    ang = inv_ref[...] * pos
    cs = jnp.cos(ang)
    sn = jnp.sin(ang)
    c_m, c_r = cs[:ROPE_HALF], cs[ROPE_HALF:]
    s_m, s_r = sn[:ROPE_HALF], sn[ROPE_HALF:]
    ones = jnp.ones((MLA_NOPE, tm), jnp.float32)
    zeros_hi = jnp.zeros((HEAD_LANES - MLA_QK, tm), jnp.float32)
    zeros_lo = jnp.zeros((MLA_NOPE, tm), jnp.float32)
    cos_k = jnp.concatenate([ones, c_m, c_m, zeros_hi], axis=0).T
    sin_k = jnp.concatenate([zeros_lo, -s_m, s_m, zeros_hi], axis=0).T
    cos_r = jnp.concatenate([c_r] * RET_HEADS, axis=0).T
    sin_r = jnp.concatenate([s_r] * RET_HEADS, axis=0).T
    scale = MLA_QK ** -0.5 * LOG2E

    cq = _dot(h, win_ref[:, _O_CQ:_O_CQ + MLA_Q_RANK])
    qt = _dot_nt(wqbt_ref[...], _bf16(_rms(cq, gq_ref[...])))
    cos_q, sin_q = c_m * scale, s_m * scale
    for hd in range(MLA_HEADS):
        base = hd * HEAD_LANES
        x1 = qt[base + MLA_NOPE:base + MLA_NOPE + ROPE_HALF]
        x2 = qt[base + MLA_NOPE + ROPE_HALF:base + MLA_QK]
        head = jnp.concatenate([qt[base:base + MLA_NOPE] * scale,
                                x1 * cos_q - x2 * sin_q, x1 * sin_q + x2 * cos_q,
                                qt[base + MLA_QK:base + HEAD_LANES]], axis=0)
        qt_ref[0, base:base + HEAD_LANES, rows] = _bf16(head)

    ckv = _dot(h, win_ref[:, _O_CKV:_O_CKV + MLA_KV_RANK])
    ckvn = _bf16(_rms(ckv, gkv_ref[...]))
    kpe = _dot(h, win_ref[:, _O_KPE:_O_KPE + HEAD_LANES])
    kpe = kpe * cos_k + _swap_rope_halves(kpe) * sin_k
    kn = _dot(ckvn, wk_ref[...])
    for hd in range(MLA_HEADS):
        sl = slice(hd * HEAD_LANES, (hd + 1) * HEAD_LANES)
        k_ref[rows, sl] = _bf16(kn[:, sl] + kpe)
    vt = _bf16(_dot_nt(wvt_ref[...], ckvn))
    per_group = tm // TQ
    for si in range(per_group):
        vt_ref[0, g * per_group + si] = vt[:, si * TQ:(si + 1) * TQ]

    for off, o_ref in ((_O_RQ, rq_ref), (_O_RK, rk_ref)):
        z = _dot(h, win_ref[:, off:off + 2 * LANES])
        z1, z2 = z[:, :LANES], z[:, LANES:]
        o_ref[rows, :LANES] = _bf16(z1 * cos_r - z2 * sin_r)
        o_ref[rows, LANES:] = _bf16(z1 * sin_r + z2 * cos_r)
    rv_ref[rows, :] = _bf16(_dot(h, win_ref[:, _O_RV:_O_RV + RET_WIDTH]))
    rg_ref[rows, :] = _bf16(_dot(h, win_ref[:, _O_RG:_O_RG + RET_WIDTH]))


def _inproj_call(x2, mod3, pos3, inv_col, gpre, win_p, gq, wqb_p, gkv, wk_p, wv_p, seq):
    t, d = x2.shape
    tm = TM_IN
    per_b = seq // tm
    tok = lambda n: pl.BlockSpec((tm, n), lambda i: (i, 0))
    mod_spec = lambda k: pl.BlockSpec((1, 1, d), lambda i: ((i // per_b) * 6 + k, 0, 0))
    tok_widths = [MLA_HEADS * HEAD_LANES, RET_HEADS * RET_QK, RET_HEADS * RET_QK, RET_WIDTH, RET_WIDTH]
    qt_spec = pl.BlockSpec((1, MLA_HEADS * HEAD_LANES, tm), lambda i: (i // per_b, 0, i % per_b))
    qt_shape = jax.ShapeDtypeStruct((t // seq, MLA_HEADS * HEAD_LANES, seq), jnp.bfloat16)
    slabs = tm // TQ
    vt_spec = pl.BlockSpec((1, slabs, MLA_WIDTH, TQ), lambda i: (i // per_b, i % per_b, 0, 0))
    vt_shape = jax.ShapeDtypeStruct((t // seq, seq // TQ, MLA_WIDTH, TQ), jnp.bfloat16)
    tok_specs = [tok(n) for n in tok_widths]
    tok_shapes = [jax.ShapeDtypeStruct((t, n), jnp.bfloat16) for n in tok_widths]
    return pl.pallas_call(
        _inproj_kernel,
        grid=(t // tm,),
        in_specs=[tok(d), mod_spec(1), mod_spec(0),
                  pl.BlockSpec((1, 1, tm), lambda i: (i, 0, 0)),
                  _const_spec(inv_col.shape), _const_spec(gpre.shape), _const_spec(win_p.shape),
                  _const_spec(gq.shape), _const_spec(wqb_p.shape), _const_spec(gkv.shape),
                  _const_spec(wk_p.shape), _const_spec(wv_p.shape)],
        out_specs=[qt_spec, tok_specs[0], vt_spec] + tok_specs[1:],
        out_shape=[qt_shape, tok_shapes[0], vt_shape] + tok_shapes[1:],
        compiler_params=pltpu.CompilerParams(dimension_semantics=("arbitrary",),
                                             vmem_limit_bytes=VMEM_LIMIT),
        name="inproj",
    )(x2, mod3, mod3, pos3, inv_col, gpre, win_p, gq, wqb_p, gkv, wk_p, wv_p)


Q_SUB = 256
PV_LAG = 2
ONES_ROWS = 16


def _attn_kernel(qt_ref, k_ref, vt_ref, bias_ref, o_ref, m_ref, acc_ref, sa_ref, sb_ref, mxa_ref, mxb_ref):
    tk, tq = TQ, AQ
    r_diag = tq // tk
    n_q = qt_ref.shape[2] // tq
    ones = jnp.ones((ONES_ROWS, tk), jnp.bfloat16)
    subs = [(hd, qo) for hd in range(2) for qo in range(0, tq, Q_SUB)]
    full = ["full"] * len(subs)

    def diag_modes(r):
        modes = []
        for _, qo in subs:
            d = qo - r * tk
            modes.append(None if d + Q_SUB <= 0 else ("full" if d >= tk else d))
        return modes

    def produce(qi, j, s_ref, mx_ref, modes, only=None):
        start = j * tk if isinstance(j, int) else pl.multiple_of(j * tk, tk)
        for i, (hd, qo) in enumerate(subs):
            if modes[i] is None or (only is not None and i != only):
                continue
            sl = slice(hd * HEAD_LANES, (hd + 1) * HEAD_LANES)
            qt = qt_ref[0, sl, qi * tq + qo:qi * tq + qo + Q_SUB]
            st = _dot(k_ref[0, pl.ds(start, tk), sl], qt)
            s_ref[i] = st
            mx_ref[i] = jnp.max(st, axis=0, keepdims=True)

    def step(prod, cons):
        for i in range(len(subs) + PV_LAG):
            if prod is not None and i < len(subs):
                produce(*prod, only=i)
            if cons is not None and i >= PV_LAG:
                consume(*cons, only=i - PV_LAG)

    def consume(j, s_ref, mx_ref, modes, only=None):
        for i, (hd, qo) in enumerate(subs):
            if modes[i] is None or (only is not None and i != only):
                continue
            qs = slice(qo, qo + Q_SUB)
            vt = vt_ref[0, j, hd * MLA_V:(hd + 1) * MLA_V, :]
            lhs = jnp.concatenate([vt, ones], axis=0)
            if modes[i] == "full":
                st = s_ref[i]
                m_cur = mx_ref[i]
            else:
                st = s_ref[i] + bias_ref[:, modes[i]:modes[i] + Q_SUB]
                m_cur = jnp.max(st, axis=0, keepdims=True)
            m_prev = m_ref[hd, :, qs]
            m_new = jnp.maximum(m_prev, m_cur)
            alpha = jnp.exp2(m_prev - m_new)
            p = _bf16(jnp.exp2(st - m_new))
            acc_ref[hd, :, qs] = alpha * acc_ref[hd, :, qs] + _dot(lhs, p)
            m_ref[hd, :, qs] = m_new

    def finalize(qi):
        outs = []
        for hd in range(2):
            acc = acc_ref[hd]
            outs.append(acc[:MLA_V] / acc[MLA_V:MLA_V + 1])
        o_ref[0, qi * tq:(qi + 1) * tq, :] = _bf16(jnp.concatenate(outs, axis=0).T)

    buf_a, buf_b = (sa_ref, mxa_ref), (sb_ref, mxb_ref)
    produce(0, 0, *buf_a, full)
    for qi in range(n_q):
        m_ref[...] = jnp.full(m_ref.shape, -jnp.inf, jnp.float32)
        acc_ref[...] = jnp.zeros(acc_ref.shape, jnp.float32)
        n_full = r_diag * qi

        def pair_body(i, carry, qi=qi, buf_a=buf_a, buf_b=buf_b):
            j = 2 * i
            step((qi, j + 1, *buf_b, full), (j, *buf_a, full))
            step((qi, j + 2, *buf_a, full), (j + 1, *buf_b, full))
            return carry

        if n_full // 2 > 0:
            lax.fori_loop(0, n_full // 2, pair_body, 0)
        if n_full % 2 == 1:
            step((qi, n_full, *buf_b, full), (n_full - 1, *buf_a, full))
            buf_a, buf_b = buf_b, buf_a
        for r in range(r_diag):
            nxt = None
            if r + 1 < r_diag:
                nxt = (qi, n_full + r + 1, *buf_b, diag_modes(r + 1))
            elif qi + 1 < n_q:
                nxt = (qi + 1, 0, *buf_b, full)
            step(nxt, (n_full + r, *buf_a, diag_modes(r)))
            buf_a, buf_b = buf_b, buf_a
        finalize(qi)


def _attn_call(qt, k, vt):
    bsz, seq, _ = k.shape
    tk, tq = TQ, AQ
    pairs = MLA_HEADS // 2
    n_sub = 2 * (tq // Q_SUB)
    idx = jnp.arange(tk)
    bias = jnp.where(idx[:, None] <= idx[None, :], 0.0, -jnp.inf).astype(jnp.float32)
    return pl.pallas_call(
        _attn_kernel,
        grid=(bsz, pairs),
        in_specs=[pl.BlockSpec((1, 2 * HEAD_LANES, seq), lambda b, p: (b, p, 0)),
                  pl.BlockSpec((1, seq, 2 * HEAD_LANES), lambda b, p: (b, 0, p)),
                  pl.BlockSpec((1, seq // tk, 2 * MLA_V, tk), lambda b, p: (b, 0, p, 0)),
                  _const_spec(bias.shape)],
        out_specs=pl.BlockSpec((1, seq, LANES), lambda b, p: (b, 0, p)),
        out_shape=jax.ShapeDtypeStruct((bsz, seq, MLA_WIDTH), jnp.bfloat16),
        scratch_shapes=[pltpu.VMEM((2, 1, tq), jnp.float32),
                        pltpu.VMEM((2, MLA_V + ONES_ROWS, tq), jnp.float32)]
        + [pltpu.VMEM((n_sub, tk, Q_SUB), jnp.float32)] * 2
        + [pltpu.VMEM((n_sub, 1, Q_SUB), jnp.float32)] * 2,
        compiler_params=pltpu.CompilerParams(
            dimension_semantics=("arbitrary", "arbitrary"),
            vmem_limit_bytes=VMEM_LIMIT),
        name="mla_attn",
    )(qt, k, vt, bias)


def _ret_kernel(q_ref, k_ref, v_ref, g_ref, hmask_ref, dec_ref, wq_ref, wk_ref, cd_ref, gain_ref,
                o_ref, state_ref):
    @pl.when(pl.program_id(1) == 0)
    def _():
        state_ref[...] = jnp.zeros(state_ref.shape, jnp.float32)

    c = RET_C
    for ci in range(q_ref.shape[1] // c):
        rows = slice(ci * c, (ci + 1) * c)
        q = q_ref[0, rows, :]
        k = k_ref[0, rows, :]
        for hd in range(RET_HEADS):
            vh = v_ref[0, rows, hd * RET_V:(hd + 1) * RET_V]
            qm = q * hmask_ref[hd]
            sc = _dot_nt(qm, k) * dec_ref[hd]
            inner = _dot(_bf16(sc), vh)
            st = state_ref[hd]
            cross = _dot(qm, _bf16(st)) * wq_ref[hd]
            u = _dot_tn(k, _bf16(vh.astype(jnp.float32) * wk_ref[hd]))
            state_ref[hd] = st * cd_ref[hd] + u
            o = inner + cross
            mu = jnp.mean(o, axis=-1, keepdims=True)
            var = jnp.mean(jnp.square(o - mu), axis=-1, keepdims=True)
            cols = slice(hd * RET_V, (hd + 1) * RET_V)
            on = (o - mu) * lax.rsqrt(var + EPS) * gain_ref[:, cols]
            g = g_ref[0, rows, cols].astype(jnp.float32)
            o_ref[0, rows, cols] = _bf16(g * jax.nn.sigmoid(g) * on)


def _ret_tables():
    c = RET_C
    f32 = jnp.float32
    log_gamma = jnp.log(1.0 - 2.0 ** (-5.0 - jnp.arange(RET_HEADS, dtype=f32)))
    idx = jnp.arange(c)
    rel = idx[:, None] - idx[None, :]
    k_scale = RET_QK ** -0.5
    dec = jnp.where(rel >= 0, jnp.exp(log_gamma[:, None, None] * jnp.maximum(rel, 0).astype(f32)), 0.0)
    dec = dec * k_scale
    w_q = jnp.exp(log_gamma[:, None] * (idx + 1).astype(f32))[:, :, None]
    w_k = (jnp.exp(log_gamma[:, None] * (c - 1 - idx).astype(f32)) * k_scale)[:, :, None]
    cd = jnp.exp(log_gamma * c)[:, None, None]
    w_q = jnp.broadcast_to(w_q, (RET_HEADS, c, RET_V))
    w_k = jnp.broadcast_to(w_k, (RET_HEADS, c, RET_V))
    cd = jnp.broadcast_to(cd, (RET_HEADS, 1, LANES))
    lane = jnp.arange(2 * LANES)
    hmask = ((lane % LANES) // RET_HALF)[None, :] == jnp.arange(RET_HEADS)[:, None]
    hmask = jnp.broadcast_to(hmask[:, None, :], (RET_HEADS, c, 2 * LANES)).astype(jnp.bfloat16)
    return hmask, dec, w_q, w_k, cd


def _ret_call(rq, rk, rv, rg, gn_gain):
    bsz, seq, _ = rq.shape
    tt = RET_T
    consts = _ret_tables() + (gn_gain,)
    tok = lambda n: pl.BlockSpec((1, tt, n), lambda b, i: (b, i, 0))
    return pl.pallas_call(
        _ret_kernel,
        grid=(bsz, seq // tt),
        in_specs=[tok(2 * LANES), tok(2 * LANES), tok(RET_WIDTH), tok(RET_WIDTH)]
        + [_const_spec(a.shape) for a in consts],
        out_specs=tok(RET_WIDTH),
        out_shape=jax.ShapeDtypeStruct((bsz, seq, RET_WIDTH), jnp.bfloat16),
        scratch_shapes=[pltpu.VMEM((RET_HEADS, 2 * LANES, RET_V), jnp.float32)],
        compiler_params=pltpu.CompilerParams(dimension_semantics=("arbitrary", "arbitrary"),
                                             vmem_limit_bytes=VMEM_LIMIT),
        name="retention",
    )(rq, rk, rv, rg, *consts)


def _outffn_kernel(x_ref, ym_ref, yr_ref, g1_ref, sh2_ref, sc2_ref, g2_ref,
                   gmla_ref, gpost_ref, gpre2_ref, gpost2_ref,
                   wout_ref, wg_ref, wu_ref, wd_ref, o_ref, h_ref, a_ref):
    groups = [slice(r, r + FF_ROWS) for r in range(0, x_ref.shape[0], FF_ROWS)]
    for rows in groups:
        ymn = _bf16(_rms(ym_ref[rows, :].astype(jnp.float32), gmla_ref[...]))
        mix = _dot(ymn, wout_ref[:MLA_WIDTH, :]) + _dot(yr_ref[rows, :], wout_ref[MLA_WIDTH:, :])
        x1 = x_ref[rows, :] + g1_ref[0] * _rms(mix, gpost_ref[...])
        o_ref[rows, :] = x1
        h_ref[rows, :] = _bf16(_rms(x1, gpre2_ref[...]) * (1.0 + sc2_ref[0]) + sh2_ref[0])
    for rows in groups:
        for ci in range(wg_ref.shape[1] // FF_CHUNK):
            cols = slice(ci * FF_CHUNK, (ci + 1) * FF_CHUNK)
            h = h_ref[rows, :]
            g = _dot(h, wg_ref[:, cols])
            u = _dot(h, wu_ref[:, cols])
            a_ref[rows, cols] = _bf16(g * jax.nn.sigmoid(g) * u)
    for rows in groups:
        f = _dot(a_ref[rows, :], wd_ref[...])
        o_ref[rows, :] = o_ref[rows, :] + g2_ref[0] * _rms(f, gpost2_ref[...])


def _outffn_call(x2, ym, yr, mod3, gmla, gpost, gpre2, gpost2, wout, wg3, wu3, wd, seq):
    t, d = x2.shape
    tm = TM_OUT
    per_b = seq // tm
    d_ff = wd.shape[0]
    tok = lambda n: pl.BlockSpec((tm, n), lambda i: (i, 0))
    mod_spec = lambda k: pl.BlockSpec((1, 1, d), lambda i: ((i // per_b) * 6 + k, 0, 0))
    return pl.pallas_call(
        _outffn_kernel,
        grid=(t // tm,),
        in_specs=[tok(d), tok(MLA_WIDTH), tok(RET_WIDTH),
                  mod_spec(2), mod_spec(3), mod_spec(4), mod_spec(5),
                  _const_spec(gmla.shape), _const_spec(gpost.shape), _const_spec(gpre2.shape),
                  _const_spec(gpost2.shape), _const_spec(wout.shape), _const_spec(wg3.shape),
                  _const_spec(wu3.shape), _const_spec(wd.shape)],
        out_specs=tok(d),
        out_shape=jax.ShapeDtypeStruct((t, d), jnp.float32),
        scratch_shapes=[pltpu.VMEM((tm, d), jnp.bfloat16), pltpu.VMEM((tm, d_ff), jnp.bfloat16)],
        compiler_params=pltpu.CompilerParams(dimension_semantics=("arbitrary",),
                                             vmem_limit_bytes=VMEM_LIMIT),
        name="outproj_ffn",
    )(x2, ym, yr, mod3, mod3, mod3, mod3, gmla, gpost, gpre2, gpost2, wout, wg3, wu3, wd)


def _relayout_w_in(w_in):
    d = w_in.shape[0]
    o = np.cumsum([0, MLA_Q_RANK, MLA_KV_RANK, MLA_ROPE, RET_HEADS * RET_QK, RET_HEADS * RET_QK,
                   RET_WIDTH, RET_WIDTH])
    cq, ckv, kpe, rq, rk, rv, rg = [w_in[:, o[i]:o[i + 1]] for i in range(7)]
    kpe = jnp.concatenate([jnp.zeros((d, MLA_NOPE), w_in.dtype), kpe,
                           jnp.zeros((d, HEAD_LANES - MLA_QK), w_in.dtype)], axis=1)

    def halves_first(w):
        return w.reshape(d, RET_HEADS, 2, RET_HALF).transpose(0, 2, 1, 3).reshape(d, RET_HEADS * RET_QK)

    return _bf16(jnp.concatenate([cq, ckv, kpe, halves_first(rq), halves_first(rk), rv, rg], axis=1))


def _relayout_w_q_b(w):
    r = w.shape[0]
    w = w.reshape(r, MLA_HEADS, MLA_QK)
    w = jnp.pad(w, ((0, 0), (0, 0), (0, HEAD_LANES - MLA_QK)))
    return _bf16(w.reshape(r, MLA_HEADS * HEAD_LANES).T)


def _relayout_w_kv_b(w):
    r = w.shape[0]
    w = w.reshape(r, MLA_HEADS, MLA_NOPE + MLA_V)
    wk = jnp.pad(w[..., :MLA_NOPE], ((0, 0), (0, 0), (0, HEAD_LANES - MLA_NOPE)))
    wv = w[..., MLA_NOPE:]
    return _bf16(wk.reshape(r, MLA_HEADS * HEAD_LANES)), _bf16(wv.reshape(r, MLA_WIDTH).T)


def _rope_inv_col():
    inv_m = ROPE_BASE ** (-jnp.arange(0, MLA_ROPE, 2, dtype=jnp.float32) / MLA_ROPE)
    inv_r = ROPE_BASE ** (-jnp.arange(0, RET_QK, 2, dtype=jnp.float32) / RET_QK)
    return jnp.concatenate([inv_m, inv_r]).reshape(N_FREQ, 1)


def kernel(x, c, positions, w_ada, b_ada, pre_norm_mix, w_in, q_a_norm, w_q_b, kv_a_norm, w_kv_b,
           mla_out_norm, ret_gn_gain, w_out, post_norm_mix, pre_norm_ffn, w_gate, w_up, w_down,
           post_norm_ffn):
    bsz, seq, d = x.shape
    t = bsz * seq
    depth = w_ada.shape[0]
    d_ff = w_gate.shape[-1]
    assert seq % TM_IN == 0 and seq % AQ == 0 and seq % RET_T == 0 and seq % TM_OUT == 0
    assert AQ % TQ == 0 and TQ % Q_SUB == 0
    assert RET_T % RET_C == 0 and d_ff % FF_CHUNK == 0
    assert IN_ROWS % TQ == 0 and TM_IN % IN_ROWS == 0 and TM_OUT % FF_ROWS == 0

    row = lambda g: g.reshape(1, -1)
    pos3 = positions.reshape(t // TM_IN, 1, TM_IN)
    inv_col = _rope_inv_col()
    x2 = x.reshape(t, d)
    for l in range(depth):
        mod3 = _mod_call(c, w_ada[l], b_ada[l]).reshape(bsz * 6, 1, d)
        wk_p, wv_p = _relayout_w_kv_b(w_kv_b[l])
        q, k, v, rq, rk, rv, rg = _inproj_call(
            x2, mod3, pos3, inv_col, row(pre_norm_mix[l]), _relayout_w_in(w_in[l]),
            row(q_a_norm[l]), _relayout_w_q_b(w_q_b[l]), row(kv_a_norm[l]), wk_p, wv_p, seq)
        b3 = lambda a: a.reshape(bsz, seq, a.shape[-1])
        y_mla = _attn_call(q, b3(k), v)
        y_ret = _ret_call(b3(rq), b3(rk), b3(rv), b3(rg), row(ret_gn_gain[l]))
        x2 = _outffn_call(
            x2, y_mla.reshape(t, MLA_WIDTH), y_ret.reshape(t, RET_WIDTH), mod3,
            row(mla_out_norm[l]), row(post_norm_mix[l]), row(pre_norm_ffn[l]), row(post_norm_ffn[l]),
            _bf16(w_out[l]), _bf16(w_gate[l]), _bf16(w_up[l]), _bf16(w_down[l]), seq)
    return x2.reshape(bsz, seq, d)
```

```python
import functools

import jax
import jax.numpy as jnp
import numpy as np
from jax import lax
from jax.experimental import pallas as pl
from jax.experimental.pallas import tpu as pltpu

MLA_HEADS = 8
MLA_NOPE = 64
MLA_ROPE = 32
MLA_V = 64
MLA_Q_RANK = 384
MLA_KV_RANK = 256
MLA_QK = MLA_NOPE + MLA_ROPE
RET_HEADS = 4
RET_QK = 64
RET_V = 128
MLA_WIDTH = MLA_HEADS * MLA_V
RET_WIDTH = RET_HEADS * RET_V
ROPE_BASE = 10000.0
EPS = 1e-6
LOG2E = 1.4426950408889634

LANES = 128
VMEM_LIMIT = 56 * 1024 * 1024

TM_IN = 1024
IN_ROWS = 1024
TQ = 512
AQ = 512
RET_C = 256
RET_T = 512
TM_OUT = 1024
FF_CHUNK = 256
FF_ROWS = 512

HEAD_LANES = LANES
ROPE_HALF = MLA_ROPE // 2
RET_HALF = RET_QK // 2
N_FREQ = ROPE_HALF + RET_HALF


def _bf16(x):
    return x.astype(jnp.bfloat16)


def _dot(a, b):
    return jnp.dot(a, b, preferred_element_type=jnp.float32)


def _dot_nt(a, b):
    return lax.dot_general(a, b, (((1,), (1,)), ((), ())), preferred_element_type=jnp.float32)


def _dot_tn(a, b):
    return lax.dot_general(a, b, (((0,), (0,)), ((), ())), preferred_element_type=jnp.float32)


def _rms(x, gain):
    return x * lax.rsqrt(jnp.mean(x * x, axis=-1, keepdims=True) + EPS) * gain


def _const_spec(shape):
    nd = len(shape)
    return pl.BlockSpec(shape, lambda *_: (0,) * nd, pipeline_mode=pl.Buffered(1))


def _mod_kernel(c_ref, w_ref, b_ref, o_ref):
    c = c_ref[...]
    a = _bf16(c * jax.nn.sigmoid(c))
    o_ref[...] = _dot(a, _bf16(w_ref[...])) + b_ref[...]


def _mod_call(c, w_ada, b_ada):
    bsz, d = c.shape
    n = w_ada.shape[1]
    tn = 1024
    return pl.pallas_call(
        _mod_kernel,
        grid=(n // tn,),
        in_specs=[pl.BlockSpec((bsz, d), lambda j: (0, 0)),
                  pl.BlockSpec((d, tn), lambda j: (0, j)),
                  pl.BlockSpec((1, tn), lambda j: (0, j))],
        out_specs=pl.BlockSpec((bsz, tn), lambda j: (0, j)),
        out_shape=jax.ShapeDtypeStruct((bsz, n), jnp.float32),
        compiler_params=pltpu.CompilerParams(dimension_semantics=("arbitrary",),
                                             vmem_limit_bytes=VMEM_LIMIT),
        name="adaln_mod",
    )(c, w_ada, b_ada.reshape(1, n))


_O_CQ = 0
_O_CKV = _O_CQ + MLA_Q_RANK
_O_KPE = _O_CKV + MLA_KV_RANK
_O_RQ = _O_KPE + HEAD_LANES
_O_RK = _O_RQ + RET_HEADS * RET_QK
_O_RV = _O_RK + RET_HEADS * RET_QK
_O_RG = _O_RV + RET_WIDTH
_IN_COLS_P = _O_RG + RET_WIDTH


def _swap_rope_halves(x):
    lane = lax.broadcasted_iota(jnp.int32, x.shape, 1)
    from_right = pltpu.roll(x, HEAD_LANES - ROPE_HALF, 1)
    from_left = pltpu.roll(x, ROPE_HALF, 1)
    return jnp.where(lane < MLA_NOPE + ROPE_HALF, from_right, from_left)


def _inproj_kernel(x_ref, sc_ref, sh_ref, pos_ref, inv_ref, gpre_ref, win_ref, gq_ref, wqbt_ref,
                   gkv_ref, wk_ref, wvt_ref,
                   qt_ref, k_ref, vt_ref, rq_ref, rk_ref, rv_ref, rg_ref):
    for g in range(x_ref.shape[0] // IN_ROWS):
        _inproj_rows(g, x_ref, sc_ref, sh_ref, pos_ref, inv_ref, gpre_ref, win_ref, gq_ref, wqbt_ref,
                     gkv_ref, wk_ref, wvt_ref, qt_ref, k_ref, vt_ref, rq_ref, rk_ref, rv_ref, rg_ref)


def _inproj_rows(g, x_ref, sc_ref, sh_ref, pos_ref, inv_ref, gpre_ref, win_ref, gq_ref, wqbt_ref,
                 gkv_ref, wk_ref, wvt_ref, qt_ref, k_ref, vt_ref, rq_ref, rk_ref, rv_ref, rg_ref):
    tm = IN_ROWS
    rows = slice(g * tm, (g + 1) * tm)
    x = x_ref[rows, :]
    h = _bf16(_rms(x, gpre_ref[...]) * (1.0 + sc_ref[0]) + sh_ref[0])

    pos = pos_ref[0, :, rows].astype(jnp.float32)
    ang = inv_ref[...] * pos
    cs = jnp.cos(ang)
    sn = jnp.sin(ang)
    c_m, c_r = cs[:ROPE_HALF], cs[ROPE_HALF:]
    s_m, s_r = sn[:ROPE_HALF], sn[ROPE_HALF:]
    ones = jnp.ones((MLA_NOPE, tm), jnp.float32)
    zeros_hi = jnp.zeros((HEAD_LANES - MLA_QK, tm), jnp.float32)
    zeros_lo = jnp.zeros((MLA_NOPE, tm), jnp.float32)
    cos_k = jnp.concatenate([ones, c_m, c_m, zeros_hi], axis=0).T
    sin_k = jnp.concatenate([zeros_lo, -s_m, s_m, zeros_hi], axis=0).T
    cos_r = jnp.concatenate([c_r] * RET_HEADS, axis=0).T
    sin_r = jnp.concatenate([s_r] * RET_HEADS, axis=0).T
    scale = MLA_QK ** -0.5 * LOG2E

    cq = _dot(h, win_ref[:, _O_CQ:_O_CQ + MLA_Q_RANK])
    qt = _dot_nt(wqbt_ref[...], _bf16(_rms(cq, gq_ref[...])))
    cos_q, sin_q = c_m * scale, s_m * scale
    for hd in range(MLA_HEADS):
        base = hd * HEAD_LANES
        x1 = qt[base + MLA_NOPE:base + MLA_NOPE + ROPE_HALF]
        x2 = qt[base + MLA_NOPE + ROPE_HALF:base + MLA_QK]
        head = jnp.concatenate([qt[base:base + MLA_NOPE] * scale,
                                x1 * cos_q - x2 * sin_q, x1 * sin_q + x2 * cos_q,
                                qt[base + MLA_QK:base + HEAD_LANES]], axis=0)
        qt_ref[0, base:base + HEAD_LANES, rows] = _bf16(head)

    ckv = _dot(h, win_ref[:, _O_CKV:_O_CKV + MLA_KV_RANK])
    ckvn = _bf16(_rms(ckv, gkv_ref[...]))
    kpe = _dot(h, win_ref[:, _O_KPE:_O_KPE + HEAD_LANES])
    kpe = kpe * cos_k + _swap_rope_halves(kpe) * sin_k
    kn = _dot(ckvn, wk_ref[...])
    for hd in range(MLA_HEADS):
        sl = slice(hd * HEAD_LANES, (hd + 1) * HEAD_LANES)
        k_ref[rows, sl] = _bf16(kn[:, sl] + kpe)
    vt = _bf16(_dot_nt(wvt_ref[...], ckvn))
    per_group = tm // TQ
    for si in range(per_group):
        vt_ref[0, g * per_group + si] = vt[:, si * TQ:(si + 1) * TQ]

    for off, o_ref in ((_O_RQ, rq_ref), (_O_RK, rk_ref)):
        z = _dot(h, win_ref[:, off:off + 2 * LANES])
        z1, z2 = z[:, :LANES], z[:, LANES:]
        o_ref[rows, :LANES] = _bf16(z1 * cos_r - z2 * sin_r)
        o_ref[rows, LANES:] = _bf16(z1 * sin_r + z2 * cos_r)
    rv_ref[rows, :] = _bf16(_dot(h, win_ref[:, _O_RV:_O_RV + RET_WIDTH]))
    rg_ref[rows, :] = _bf16(_dot(h, win_ref[:, _O_RG:_O_RG + RET_WIDTH]))


def _inproj_call(x2, mod3, pos3, inv_col, gpre, win_p, gq, wqb_p, gkv, wk_p, wv_p, seq):
    t, d = x2.shape
    tm = TM_IN
    per_b = seq // tm
    tok = lambda n: pl.BlockSpec((tm, n), lambda i: (i, 0))
    mod_spec = lambda k: pl.BlockSpec((1, 1, d), lambda i: ((i // per_b) * 6 + k, 0, 0))
    tok_widths = [MLA_HEADS * HEAD_LANES, RET_HEADS * RET_QK, RET_HEADS * RET_QK, RET_WIDTH, RET_WIDTH]
    qt_spec = pl.BlockSpec((1, MLA_HEADS * HEAD_LANES, tm), lambda i: (i // per_b, 0, i % per_b))
    qt_shape = jax.ShapeDtypeStruct((t // seq, MLA_HEADS * HEAD_LANES, seq), jnp.bfloat16)
    slabs = tm // TQ
    vt_spec = pl.BlockSpec((1, slabs, MLA_WIDTH, TQ), lambda i: (i // per_b, i % per_b, 0, 0))
    vt_shape = jax.ShapeDtypeStruct((t // seq, seq // TQ, MLA_WIDTH, TQ), jnp.bfloat16)
    tok_specs = [tok(n) for n in tok_widths]
    tok_shapes = [jax.ShapeDtypeStruct((t, n), jnp.bfloat16) for n in tok_widths]
    return pl.pallas_call(
        _inproj_kernel,
        grid=(t // tm,),
        in_specs=[tok(d), mod_spec(1), mod_spec(0),
                  pl.BlockSpec((1, 1, tm), lambda i: (i, 0, 0)),
                  _const_spec(inv_col.shape), _const_spec(gpre.shape), _const_spec(win_p.shape),
                  _const_spec(gq.shape), _const_spec(wqb_p.shape), _const_spec(gkv.shape),
                  _const_spec(wk_p.shape), _const_spec(wv_p.shape)],
        out_specs=[qt_spec, tok_specs[0], vt_spec] + tok_specs[1:],
        out_shape=[qt_shape, tok_shapes[0], vt_shape] + tok_shapes[1:],
        compiler_params=pltpu.CompilerParams(dimension_semantics=("arbitrary",),
                                             vmem_limit_bytes=VMEM_LIMIT),
        name="inproj",
    )(x2, mod3, mod3, pos3, inv_col, gpre, win_p, gq, wqb_p, gkv, wk_p, wv_p)


Q_SUB = 256
PV_LAG = 2
ONES_ROWS = 16


def _attn_kernel(qt_ref, k_ref, vt_ref, bias_ref, o_ref, m_ref, acc_ref, sa_ref, sb_ref, mxa_ref, mxb_ref):
    tk, tq = TQ, AQ
    r_diag = tq // tk
    n_q = qt_ref.shape[2] // tq
    ones = jnp.ones((ONES_ROWS, tk), jnp.bfloat16)
    subs = [(hd, qo) for hd in range(2) for qo in range(0, tq, Q_SUB)]
    full = ["full"] * len(subs)

    def diag_modes(r):
        modes = []
        for _, qo in subs:
            d = qo - r * tk
            modes.append(None if d + Q_SUB <= 0 else ("full" if d >= tk else d))
        return modes

    def produce(qi, j, s_ref, mx_ref, modes, only=None):
        start = j * tk if isinstance(j, int) else pl.multiple_of(j * tk, tk)
        for i, (hd, qo) in enumerate(subs):
            if modes[i] is None or (only is not None and i != only):
                continue
            sl = slice(hd * HEAD_LANES, (hd + 1) * HEAD_LANES)
            qt = qt_ref[0, sl, qi * tq + qo:qi * tq + qo + Q_SUB]
            st = _dot(k_ref[0, pl.ds(start, tk), sl], qt)
            if modes[i] != "full":
                st = st + bias_ref[:, modes[i]:modes[i] + Q_SUB]
            s_ref[i] = st
            mx_ref[i] = jnp.max(st, axis=0, keepdims=True)

    def step(prod, cons):
        for i in range(len(subs) + PV_LAG):
            if prod is not None and i < len(subs):
                produce(*prod, only=i)
            if cons is not None and i >= PV_LAG:
                consume(*cons, only=i - PV_LAG)

    def consume(j, s_ref, mx_ref, modes, only=None):
        for i, (hd, qo) in enumerate(subs):
            if modes[i] is None or (only is not None and i != only):
                continue
            qs = slice(qo, qo + Q_SUB)
            vt = vt_ref[0, j, hd * MLA_V:(hd + 1) * MLA_V, :]
            lhs = jnp.concatenate([vt, ones], axis=0)
            st = s_ref[i]
            m_prev = m_ref[hd, :, qs]
            m_new = jnp.maximum(m_prev, mx_ref[i])
            alpha = jnp.exp2(m_prev - m_new)
            p = _bf16(jnp.exp2(st - m_new))
            acc_ref[hd, :, qs] = alpha * acc_ref[hd, :, qs] + _dot(lhs, p)
            m_ref[hd, :, qs] = m_new

    def finalize(qi):
        outs = []
        for hd in range(2):
            acc = acc_ref[hd]
            outs.append(acc[:MLA_V] / acc[MLA_V:MLA_V + 1])
        o_ref[0, qi * tq:(qi + 1) * tq, :] = _bf16(jnp.concatenate(outs, axis=0).T)

    def tile_modes(qi, j):
        n_full = r_diag * qi
        return full if j < n_full else diag_modes(j - n_full)

    buf_a, buf_b = (sa_ref, mxa_ref), (sb_ref, mxb_ref)
    produce(0, 0, *buf_a, tile_modes(0, 0))
    for qi in range(n_q):
        m_ref[...] = jnp.full(m_ref.shape, -jnp.inf, jnp.float32)
        acc_ref[...] = jnp.zeros(acc_ref.shape, jnp.float32)
        n_full = r_diag * qi
        last = n_full + r_diag - 1
        n_pairs = max(0, (n_full - 1) // 2)

        def pair_body(i, carry, qi=qi, buf_a=buf_a, buf_b=buf_b):
            j = 2 * i
            step((qi, j + 1, *buf_b, full), (j, *buf_a, full))
            step((qi, j + 2, *buf_a, full), (j + 1, *buf_b, full))
            return carry

        if n_pairs > 0:
            lax.fori_loop(0, n_pairs, pair_body, 0)
        for j in range(2 * n_pairs, last + 1):
            if j < last:
                nxt = (qi, j + 1, *buf_b, tile_modes(qi, j + 1))
            elif qi + 1 < n_q:
                nxt = (qi + 1, 0, *buf_b, tile_modes(qi + 1, 0))
            else:
                nxt = None
            step(nxt, (j, *buf_a, tile_modes(qi, j)))
            buf_a, buf_b = buf_b, buf_a
        finalize(qi)


def _attn_call(qt, k, vt):
    bsz, seq, _ = k.shape
    tk, tq = TQ, AQ
    pairs = MLA_HEADS // 2
    n_sub = 2 * (tq // Q_SUB)
    idx = jnp.arange(tk)
    bias = jnp.where(idx[:, None] <= idx[None, :], 0.0, -jnp.inf).astype(jnp.float32)
    return pl.pallas_call(
        _attn_kernel,
        grid=(bsz, pairs),
        in_specs=[pl.BlockSpec((1, 2 * HEAD_LANES, seq), lambda b, p: (b, p, 0)),
                  pl.BlockSpec((1, seq, 2 * HEAD_LANES), lambda b, p: (b, 0, p)),
                  pl.BlockSpec((1, seq // tk, 2 * MLA_V, tk), lambda b, p: (b, 0, p, 0)),
                  _const_spec(bias.shape)],
        out_specs=pl.BlockSpec((1, seq, LANES), lambda b, p: (b, 0, p)),
        out_shape=jax.ShapeDtypeStruct((bsz, seq, MLA_WIDTH), jnp.bfloat16),
        scratch_shapes=[pltpu.VMEM((2, 1, tq), jnp.float32),
                        pltpu.VMEM((2, MLA_V + ONES_ROWS, tq), jnp.float32)]
        + [pltpu.VMEM((n_sub, tk, Q_SUB), jnp.float32)] * 2
        + [pltpu.VMEM((n_sub, 1, Q_SUB), jnp.float32)] * 2,
        compiler_params=pltpu.CompilerParams(
            dimension_semantics=("arbitrary", "arbitrary"),
            vmem_limit_bytes=VMEM_LIMIT),
        name="mla_attn",
    )(qt, k, vt, bias)


def _ret_kernel(q_ref, k_ref, v_ref, g_ref, hmask_ref, dec_ref, wq_ref, wk_ref, cd_ref, gain_ref,
                o_ref, state_ref):
    @pl.when(pl.program_id(1) == 0)
    def _():
        state_ref[...] = jnp.zeros(state_ref.shape, jnp.float32)

    c = RET_C
    for ci in range(q_ref.shape[1] // c):
        rows = slice(ci * c, (ci + 1) * c)
        q = q_ref[0, rows, :]
        k = k_ref[0, rows, :]
        for hd in range(RET_HEADS):
            vh = v_ref[0, rows, hd * RET_V:(hd + 1) * RET_V]
            qm = q * hmask_ref[hd]
            sc = _dot_nt(qm, k) * dec_ref[hd]
            inner = _dot(_bf16(sc), vh)
            st = state_ref[hd]
            cross = _dot(qm, _bf16(st)) * wq_ref[hd]
            u = _dot_tn(k, _bf16(vh.astype(jnp.float32) * wk_ref[hd]))
            state_ref[hd] = st * cd_ref[hd] + u
            o = inner + cross
            mu = jnp.mean(o, axis=-1, keepdims=True)
            var = jnp.mean(jnp.square(o - mu), axis=-1, keepdims=True)
            cols = slice(hd * RET_V, (hd + 1) * RET_V)
            on = (o - mu) * lax.rsqrt(var + EPS) * gain_ref[:, cols]
            g = g_ref[0, rows, cols].astype(jnp.float32)
            o_ref[0, rows, cols] = _bf16(g * jax.nn.sigmoid(g) * on)


def _ret_tables():
    c = RET_C
    f32 = jnp.float32
    log_gamma = jnp.log(1.0 - 2.0 ** (-5.0 - jnp.arange(RET_HEADS, dtype=f32)))
    idx = jnp.arange(c)
    rel = idx[:, None] - idx[None, :]
    k_scale = RET_QK ** -0.5
    dec = jnp.where(rel >= 0, jnp.exp(log_gamma[:, None, None] * jnp.maximum(rel, 0).astype(f32)), 0.0)
    dec = dec * k_scale
    w_q = jnp.exp(log_gamma[:, None] * (idx + 1).astype(f32))[:, :, None]
    w_k = (jnp.exp(log_gamma[:, None] * (c - 1 - idx).astype(f32)) * k_scale)[:, :, None]
    cd = jnp.exp(log_gamma * c)[:, None, None]
    w_q = jnp.broadcast_to(w_q, (RET_HEADS, c, RET_V))
    w_k = jnp.broadcast_to(w_k, (RET_HEADS, c, RET_V))
    cd = jnp.broadcast_to(cd, (RET_HEADS, 1, LANES))
    lane = jnp.arange(2 * LANES)
    hmask = ((lane % LANES) // RET_HALF)[None, :] == jnp.arange(RET_HEADS)[:, None]
    hmask = jnp.broadcast_to(hmask[:, None, :], (RET_HEADS, c, 2 * LANES)).astype(jnp.bfloat16)
    return hmask, dec, w_q, w_k, cd


def _ret_call(rq, rk, rv, rg, gn_gain):
    bsz, seq, _ = rq.shape
    tt = RET_T
    consts = _ret_tables() + (gn_gain,)
    tok = lambda n: pl.BlockSpec((1, tt, n), lambda b, i: (b, i, 0))
    return pl.pallas_call(
        _ret_kernel,
        grid=(bsz, seq // tt),
        in_specs=[tok(2 * LANES), tok(2 * LANES), tok(RET_WIDTH), tok(RET_WIDTH)]
        + [_const_spec(a.shape) for a in consts],
        out_specs=tok(RET_WIDTH),
        out_shape=jax.ShapeDtypeStruct((bsz, seq, RET_WIDTH), jnp.bfloat16),
        scratch_shapes=[pltpu.VMEM((RET_HEADS, 2 * LANES, RET_V), jnp.float32)],
        compiler_params=pltpu.CompilerParams(dimension_semantics=("arbitrary", "arbitrary"),
                                             vmem_limit_bytes=VMEM_LIMIT),
        name="retention",
    )(rq, rk, rv, rg, *consts)


def _outffn_kernel(x_ref, ym_ref, yr_ref, g1_ref, sh2_ref, sc2_ref, g2_ref,
                   gmla_ref, gpost_ref, gpre2_ref, gpost2_ref,
                   wout_ref, wg_ref, wu_ref, wd_ref, o_ref, h_ref, a_ref):
    groups = [slice(r, r + FF_ROWS) for r in range(0, x_ref.shape[0], FF_ROWS)]
    for rows in groups:
        ymn = _bf16(_rms(ym_ref[rows, :].astype(jnp.float32), gmla_ref[...]))
        mix = _dot(ymn, wout_ref[:MLA_WIDTH, :]) + _dot(yr_ref[rows, :], wout_ref[MLA_WIDTH:, :])
        x1 = x_ref[rows, :] + g1_ref[0] * _rms(mix, gpost_ref[...])
        o_ref[rows, :] = x1
        h_ref[rows, :] = _bf16(_rms(x1, gpre2_ref[...]) * (1.0 + sc2_ref[0]) + sh2_ref[0])
    for rows in groups:
        for ci in range(wg_ref.shape[1] // FF_CHUNK):
            cols = slice(ci * FF_CHUNK, (ci + 1) * FF_CHUNK)
            h = h_ref[rows, :]
            g = _dot(h, wg_ref[:, cols])
            u = _dot(h, wu_ref[:, cols])
            a_ref[rows, cols] = _bf16(g * jax.nn.sigmoid(g) * u)
    for rows in groups:
        f = _dot(a_ref[rows, :], wd_ref[...])
        o_ref[rows, :] = o_ref[rows, :] + g2_ref[0] * _rms(f, gpost2_ref[...])


def _outffn_call(x2, ym, yr, mod3, gmla, gpost, gpre2, gpost2, wout, wg3, wu3, wd, seq):
    t, d = x2.shape
    tm = TM_OUT
    per_b = seq // tm
    d_ff = wd.shape[0]
    tok = lambda n: pl.BlockSpec((tm, n), lambda i: (i, 0))
    mod_spec = lambda k: pl.BlockSpec((1, 1, d), lambda i: ((i // per_b) * 6 + k, 0, 0))
    return pl.pallas_call(
        _outffn_kernel,
        grid=(t // tm,),
        in_specs=[tok(d), tok(MLA_WIDTH), tok(RET_WIDTH),
                  mod_spec(2), mod_spec(3), mod_spec(4), mod_spec(5),
                  _const_spec(gmla.shape), _const_spec(gpost.shape), _const_spec(gpre2.shape),
                  _const_spec(gpost2.shape), _const_spec(wout.shape), _const_spec(wg3.shape),
                  _const_spec(wu3.shape), _const_spec(wd.shape)],
        out_specs=tok(d),
        out_shape=jax.ShapeDtypeStruct((t, d), jnp.float32),
        scratch_shapes=[pltpu.VMEM((tm, d), jnp.bfloat16), pltpu.VMEM((tm, d_ff), jnp.bfloat16)],
        compiler_params=pltpu.CompilerParams(dimension_semantics=("arbitrary",),
                                             vmem_limit_bytes=VMEM_LIMIT),
        name="outproj_ffn",
    )(x2, ym, yr, mod3, mod3, mod3, mod3, gmla, gpost, gpre2, gpost2, wout, wg3, wu3, wd)


def _relayout_w_in(w_in):
    d = w_in.shape[0]
    o = np.cumsum([0, MLA_Q_RANK, MLA_KV_RANK, MLA_ROPE, RET_HEADS * RET_QK, RET_HEADS * RET_QK,
                   RET_WIDTH, RET_WIDTH])
    cq, ckv, kpe, rq, rk, rv, rg = [w_in[:, o[i]:o[i + 1]] for i in range(7)]
    kpe = jnp.concatenate([jnp.zeros((d, MLA_NOPE), w_in.dtype), kpe,
                           jnp.zeros((d, HEAD_LANES - MLA_QK), w_in.dtype)], axis=1)

    def halves_first(w):
        return w.reshape(d, RET_HEADS, 2, RET_HALF).transpose(0, 2, 1, 3).reshape(d, RET_HEADS * RET_QK)

    return _bf16(jnp.concatenate([cq, ckv, kpe, halves_first(rq), halves_first(rk), rv, rg], axis=1))


def _relayout_w_q_b(w):
    r = w.shape[0]
    w = w.reshape(r, MLA_HEADS, MLA_QK)
    w = jnp.pad(w, ((0, 0), (0, 0), (0, HEAD_LANES - MLA_QK)))
    return _bf16(w.reshape(r, MLA_HEADS * HEAD_LANES).T)


def _relayout_w_kv_b(w):
    r = w.shape[0]
    w = w.reshape(r, MLA_HEADS, MLA_NOPE + MLA_V)
    wk = jnp.pad(w[..., :MLA_NOPE], ((0, 0), (0, 0), (0, HEAD_LANES - MLA_NOPE)))
    wv = w[..., MLA_NOPE:]
    return _bf16(wk.reshape(r, MLA_HEADS * HEAD_LANES)), _bf16(wv.reshape(r, MLA_WIDTH).T)


def _rope_inv_col():
    inv_m = ROPE_BASE ** (-jnp.arange(0, MLA_ROPE, 2, dtype=jnp.float32) / MLA_ROPE)
    inv_r = ROPE_BASE ** (-jnp.arange(0, RET_QK, 2, dtype=jnp.float32) / RET_QK)
    return jnp.concatenate([inv_m, inv_r]).reshape(N_FREQ, 1)


def kernel(x, c, positions, w_ada, b_ada, pre_norm_mix, w_in, q_a_norm, w_q_b, kv_a_norm, w_kv_b,
           mla_out_norm, ret_gn_gain, w_out, post_norm_mix, pre_norm_ffn, w_gate, w_up, w_down,
           post_norm_ffn):
    bsz, seq, d = x.shape
    t = bsz * seq
    depth = w_ada.shape[0]
    d_ff = w_gate.shape[-1]
    assert seq % TM_IN == 0 and seq % AQ == 0 and seq % RET_T == 0 and seq % TM_OUT == 0
    assert AQ % TQ == 0 and TQ % Q_SUB == 0
    assert RET_T % RET_C == 0 and d_ff % FF_CHUNK == 0
    assert IN_ROWS % TQ == 0 and TM_IN % IN_ROWS == 0 and TM_OUT % FF_ROWS == 0

    row = lambda g: g.reshape(1, -1)
    pos3 = positions.reshape(t // TM_IN, 1, TM_IN)
    inv_col = _rope_inv_col()
    x2 = x.reshape(t, d)
    for l in range(depth):
        mod3 = _mod_call(c, w_ada[l], b_ada[l]).reshape(bsz * 6, 1, d)
        wk_p, wv_p = _relayout_w_kv_b(w_kv_b[l])
        q, k, v, rq, rk, rv, rg = _inproj_call(
            x2, mod3, pos3, inv_col, row(pre_norm_mix[l]), _relayout_w_in(w_in[l]),
            row(q_a_norm[l]), _relayout_w_q_b(w_q_b[l]), row(kv_a_norm[l]), wk_p, wv_p, seq)
        b3 = lambda a: a.reshape(bsz, seq, a.shape[-1])
        y_mla = _attn_call(q, b3(k), v)
        y_ret = _ret_call(b3(rq), b3(rk), b3(rv), b3(rg), row(ret_gn_gain[l]))
        x2 = _outffn_call(
            x2, y_mla.reshape(t, MLA_WIDTH), y_ret.reshape(t, RET_WIDTH), mod3,
            row(mla_out_norm[l]), row(post_norm_mix[l]), row(pre_norm_ffn[l]), row(post_norm_ffn[l]),
            _bf16(w_out[l]), _bf16(w_gate[l]), _bf16(w_up[l]), _bf16(w_down[l]), seq)
    return x2.reshape(bsz, seq, d)
```

```python
import functools

import jax
import jax.numpy as jnp
import numpy as np
from jax import lax
from jax.experimental import pallas as pl
from jax.experimental.pallas import tpu as pltpu

MLA_HEADS = 8
MLA_NOPE = 64
MLA_ROPE = 32
MLA_V = 64
MLA_Q_RANK = 384
MLA_KV_RANK = 256
MLA_QK = MLA_NOPE + MLA_ROPE
RET_HEADS = 4
RET_QK = 64
RET_V = 128
MLA_WIDTH = MLA_HEADS * MLA_V
RET_WIDTH = RET_HEADS * RET_V
ROPE_BASE = 10000.0
EPS = 1e-6
LOG2E = 1.4426950408889634

LANES = 128
VMEM_LIMIT = 56 * 1024 * 1024

TM_IN = 1024
IN_ROWS = 1024
TQ = 512
AQ = 512
RET_C = 256
RET_T = 512
TM_OUT = 1024
FF_CHUNK = 256
FF_ROWS = 512

HEAD_LANES = LANES
ROPE_HALF = MLA_ROPE // 2
RET_HALF = RET_QK // 2
N_FREQ = ROPE_HALF + RET_HALF


def _bf16(x):
    return x.astype(jnp.bfloat16)


def _dot(a, b):
    return jnp.dot(a, b, preferred_element_type=jnp.float32)


def _dot_nt(a, b):
    return lax.dot_general(a, b, (((1,), (1,)), ((), ())), preferred_element_type=jnp.float32)


def _dot_tn(a, b):
    return lax.dot_general(a, b, (((0,), (0,)), ((), ())), preferred_element_type=jnp.float32)


def _rms(x, gain):
    return x * lax.rsqrt(jnp.mean(x * x, axis=-1, keepdims=True) + EPS) * gain


def _const_spec(shape):
    nd = len(shape)
    return pl.BlockSpec(shape, lambda *_: (0,) * nd, pipeline_mode=pl.Buffered(1))


def _mod_kernel(c_ref, w_ref, b_ref, o_ref):
    c = c_ref[...]
    a = _bf16(c * jax.nn.sigmoid(c))
    o_ref[...] = _dot(a, _bf16(w_ref[...])) + b_ref[...]


def _mod_call(c, w_ada, b_ada):
    bsz, d = c.shape
    n = w_ada.shape[1]
    tn = 1024
    return pl.pallas_call(
        _mod_kernel,
        grid=(n // tn,),
        in_specs=[pl.BlockSpec((bsz, d), lambda j: (0, 0)),
                  pl.BlockSpec((d, tn), lambda j: (0, j)),
                  pl.BlockSpec((1, tn), lambda j: (0, j))],
        out_specs=pl.BlockSpec((bsz, tn), lambda j: (0, j)),
        out_shape=jax.ShapeDtypeStruct((bsz, n), jnp.float32),
        compiler_params=pltpu.CompilerParams(dimension_semantics=("arbitrary",),
                                             vmem_limit_bytes=VMEM_LIMIT),
        name="adaln_mod",
    )(c, w_ada, b_ada.reshape(1, n))


_O_CQ = 0
_O_CKV = _O_CQ + MLA_Q_RANK
_O_KPE = _O_CKV + MLA_KV_RANK
_O_RQ = _O_KPE + HEAD_LANES
_O_RK = _O_RQ + RET_HEADS * RET_QK
_O_RV = _O_RK + RET_HEADS * RET_QK
_O_RG = _O_RV + RET_WIDTH
_IN_COLS_P = _O_RG + RET_WIDTH


def _swap_rope_halves(x):
    lane = lax.broadcasted_iota(jnp.int32, x.shape, 1)
    from_right = pltpu.roll(x, HEAD_LANES - ROPE_HALF, 1)
    from_left = pltpu.roll(x, ROPE_HALF, 1)
    return jnp.where(lane < MLA_NOPE + ROPE_HALF, from_right, from_left)


def _inproj_kernel(x_ref, sc_ref, sh_ref, pos_ref, inv_ref, gpre_ref, win_ref, gq_ref, wqbt_ref,
                   gkv_ref, wk_ref, wvt_ref,
                   qt_ref, k_ref, vt_ref, rq_ref, rk_ref, rv_ref, rg_ref):
    for g in range(x_ref.shape[0] // IN_ROWS):
        _inproj_rows(g, x_ref, sc_ref, sh_ref, pos_ref, inv_ref, gpre_ref, win_ref, gq_ref, wqbt_ref,
                     gkv_ref, wk_ref, wvt_ref, qt_ref, k_ref, vt_ref, rq_ref, rk_ref, rv_ref, rg_ref)


def _inproj_rows(g, x_ref, sc_ref, sh_ref, pos_ref, inv_ref, gpre_ref, win_ref, gq_ref, wqbt_ref,
                 gkv_ref, wk_ref, wvt_ref, qt_ref, k_ref, vt_ref, rq_ref, rk_ref, rv_ref, rg_ref):
    tm = IN_ROWS
    rows = slice(g * tm, (g + 1) * tm)
    x = x_ref[rows, :]
    h = _bf16(_rms(x, gpre_ref[...]) * (1.0 + sc_ref[0]) + sh_ref[0])

    pos = pos_ref[0, :, rows].astype(jnp.float32)
    ang = inv_ref[...] * pos
    cs = jnp.cos(ang)
    sn = jnp.sin(ang)
    c_m, c_r = cs[:ROPE_HALF], cs[ROPE_HALF:]
    s_m, s_r = sn[:ROPE_HALF], sn[ROPE_HALF:]
    ones = jnp.ones((MLA_NOPE, tm), jnp.float32)
    zeros_hi = jnp.zeros((HEAD_LANES - MLA_QK, tm), jnp.float32)
    zeros_lo = jnp.zeros((MLA_NOPE, tm), jnp.float32)
    cos_k = jnp.concatenate([ones, c_m, c_m, zeros_hi], axis=0).T
    sin_k = jnp.concatenate([zeros_lo, -s_m, s_m, zeros_hi], axis=0).T
    cos_r = jnp.concatenate([c_r] * RET_HEADS, axis=0).T
    sin_r = jnp.concatenate([s_r] * RET_HEADS, axis=0).T
    scale = MLA_QK ** -0.5 * LOG2E

    cq = _dot(h, win_ref[:, _O_CQ:_O_CQ + MLA_Q_RANK])
    qt = _dot_nt(wqbt_ref[...], _bf16(_rms(cq, gq_ref[...])))
    cos_q, sin_q = c_m * scale, s_m * scale
    for hd in range(MLA_HEADS):
        base = hd * HEAD_LANES
        x1 = qt[base + MLA_NOPE:base + MLA_NOPE + ROPE_HALF]
        x2 = qt[base + MLA_NOPE + ROPE_HALF:base + MLA_QK]
        head = jnp.concatenate([qt[base:base + MLA_NOPE] * scale,
                                x1 * cos_q - x2 * sin_q, x1 * sin_q + x2 * cos_q,
                                qt[base + MLA_QK:base + HEAD_LANES]], axis=0)
        qt_ref[0, base:base + HEAD_LANES, rows] = _bf16(head)

    ckv = _dot(h, win_ref[:, _O_CKV:_O_CKV + MLA_KV_RANK])
    ckvn = _bf16(_rms(ckv, gkv_ref[...]))
    kpe = _dot(h, win_ref[:, _O_KPE:_O_KPE + HEAD_LANES])
    kpe = kpe * cos_k + _swap_rope_halves(kpe) * sin_k
    kn = _dot(ckvn, wk_ref[...])
    for hd in range(MLA_HEADS):
        sl = slice(hd * HEAD_LANES, (hd + 1) * HEAD_LANES)
        k_ref[rows, sl] = _bf16(kn[:, sl] + kpe)
    vt = _bf16(_dot_nt(wvt_ref[...], ckvn))
    per_group = tm // TQ
    for si in range(per_group):
        vt_ref[0, g * per_group + si] = vt[:, si * TQ:(si + 1) * TQ]

    for off, o_ref in ((_O_RQ, rq_ref), (_O_RK, rk_ref)):
        z = _dot(h, win_ref[:, off:off + 2 * LANES])
        z1, z2 = z[:, :LANES], z[:, LANES:]
        o_ref[rows, :LANES] = _bf16(z1 * cos_r - z2 * sin_r)
        o_ref[rows, LANES:] = _bf16(z1 * sin_r + z2 * cos_r)
    rv_ref[rows, :] = _bf16(_dot(h, win_ref[:, _O_RV:_O_RV + RET_WIDTH]))
    rg_ref[rows, :] = _bf16(_dot(h, win_ref[:, _O_RG:_O_RG + RET_WIDTH]))


def _inproj_call(x2, mod3, pos3, inv_col, gpre, win_p, gq, wqb_p, gkv, wk_p, wv_p, seq):
    t, d = x2.shape
    tm = TM_IN
    per_b = seq // tm
    tok = lambda n: pl.BlockSpec((tm, n), lambda i: (i, 0))
    mod_spec = lambda k: pl.BlockSpec((1, 1, d), lambda i: ((i // per_b) * 6 + k, 0, 0))
    tok_widths = [MLA_HEADS * HEAD_LANES, RET_HEADS * RET_QK, RET_HEADS * RET_QK, RET_WIDTH, RET_WIDTH]
    qt_spec = pl.BlockSpec((1, MLA_HEADS * HEAD_LANES, tm), lambda i: (i // per_b, 0, i % per_b))
    qt_shape = jax.ShapeDtypeStruct((t // seq, MLA_HEADS * HEAD_LANES, seq), jnp.bfloat16)
    slabs = tm // TQ
    vt_spec = pl.BlockSpec((1, slabs, MLA_WIDTH, TQ), lambda i: (i // per_b, i % per_b, 0, 0))
    vt_shape = jax.ShapeDtypeStruct((t // seq, seq // TQ, MLA_WIDTH, TQ), jnp.bfloat16)
    tok_specs = [tok(n) for n in tok_widths]
    tok_shapes = [jax.ShapeDtypeStruct((t, n), jnp.bfloat16) for n in tok_widths]
    return pl.pallas_call(
        _inproj_kernel,
        grid=(t // tm,),
        in_specs=[tok(d), mod_spec(1), mod_spec(0),
                  pl.BlockSpec((1, 1, tm), lambda i: (i, 0, 0)),
                  _const_spec(inv_col.shape), _const_spec(gpre.shape), _const_spec(win_p.shape),
                  _const_spec(gq.shape), _const_spec(wqb_p.shape), _const_spec(gkv.shape),
                  _const_spec(wk_p.shape), _const_spec(wv_p.shape)],
        out_specs=[qt_spec, tok_specs[0], vt_spec] + tok_specs[1:],
        out_shape=[qt_shape, tok_shapes[0], vt_shape] + tok_shapes[1:],
        compiler_params=pltpu.CompilerParams(dimension_semantics=("arbitrary",),
                                             vmem_limit_bytes=VMEM_LIMIT),
        name="inproj",
    )(x2, mod3, mod3, pos3, inv_col, gpre, win_p, gq, wqb_p, gkv, wk_p, wv_p)


Q_SUB = 256
PV_LAG = 2
ONES_ROWS = 16


def _attn_kernel(qt_ref, k_ref, vt_ref, bias_ref, o_ref, m_ref, acc_ref, sa_ref, sb_ref, mxa_ref, mxb_ref):
    tk, tq = TQ, AQ
    r_diag = tq // tk
    n_q = qt_ref.shape[2] // tq
    ones = jnp.ones((ONES_ROWS, tk), jnp.bfloat16)
    subs = [(hd, qo) for hd in range(2) for qo in range(0, tq, Q_SUB)]
    full = ["full"] * len(subs)

    def diag_modes(r):
        modes = []
        for _, qo in subs:
            d = qo - r * tk
            modes.append(None if d + Q_SUB <= 0 else ("full" if d >= tk else d))
        return modes

    def key_extent(mode):
        return tk if mode == "full" else min(tk, mode + Q_SUB)

    def produce(qi, j, s_ref, mx_ref, modes, only=None):
        start = j * tk if isinstance(j, int) else pl.multiple_of(j * tk, tk)
        for i, (hd, qo) in enumerate(subs):
            if modes[i] is None or (only is not None and i != only):
                continue
            sl = slice(hd * HEAD_LANES, (hd + 1) * HEAD_LANES)
            qt = qt_ref[0, sl, qi * tq + qo:qi * tq + qo + Q_SUB]
            kx = key_extent(modes[i])
            st = _dot(k_ref[0, pl.ds(start, kx), sl], qt)
            if modes[i] != "full":
                st = st + bias_ref[:kx, modes[i]:modes[i] + Q_SUB]
            s_ref[i, :kx] = st
            mx_ref[i] = jnp.max(st, axis=0, keepdims=True)

    def step(prod, cons):
        for i in range(len(subs) + PV_LAG):
            if prod is not None and i < len(subs):
                produce(*prod, only=i)
            if cons is not None and i >= PV_LAG:
                consume(*cons, only=i - PV_LAG)

    def consume(j, s_ref, mx_ref, modes, only=None):
        for i, (hd, qo) in enumerate(subs):
            if modes[i] is None or (only is not None and i != only):
                continue
            qs = slice(qo, qo + Q_SUB)
            kx = key_extent(modes[i])
            vt = vt_ref[0, j, hd * MLA_V:(hd + 1) * MLA_V, :kx]
            lhs = jnp.concatenate([vt, ones[:, :kx]], axis=0)
            st = s_ref[i, :kx]
            m_prev = m_ref[hd, :, qs]
            m_new = jnp.maximum(m_prev, mx_ref[i])
            alpha = jnp.exp2(m_prev - m_new)
            p = _bf16(jnp.exp2(st - m_new))
            acc_ref[hd, :, qs] = alpha * acc_ref[hd, :, qs] + _dot(lhs, p)
            m_ref[hd, :, qs] = m_new

    def finalize(qi):
        outs = []
        for hd in range(2):
            acc = acc_ref[hd]
            outs.append(acc[:MLA_V] / acc[MLA_V:MLA_V + 1])
        o_ref[0, qi * tq:(qi + 1) * tq, :] = _bf16(jnp.concatenate(outs, axis=0).T)

    def tile_modes(qi, j):
        n_full = r_diag * qi
        return full if j < n_full else diag_modes(j - n_full)

    buf_a, buf_b = (sa_ref, mxa_ref), (sb_ref, mxb_ref)
    produce(0, 0, *buf_a, tile_modes(0, 0))
    for qi in range(n_q):
        m_ref[...] = jnp.full(m_ref.shape, -jnp.inf, jnp.float32)
        acc_ref[...] = jnp.zeros(acc_ref.shape, jnp.float32)
        n_full = r_diag * qi
        last = n_full + r_diag - 1
        n_pairs = max(0, (n_full - 1) // 2)

        def pair_body(i, carry, qi=qi, buf_a=buf_a, buf_b=buf_b):
            j = 2 * i
            step((qi, j + 1, *buf_b, full), (j, *buf_a, full))
            step((qi, j + 2, *buf_a, full), (j + 1, *buf_b, full))
            return carry

        if n_pairs > 0:
            lax.fori_loop(0, n_pairs, pair_body, 0)
        for j in range(2 * n_pairs, last + 1):
            if j < last:
                nxt = (qi, j + 1, *buf_b, tile_modes(qi, j + 1))
            elif qi + 1 < n_q:
                nxt = (qi + 1, 0, *buf_b, tile_modes(qi + 1, 0))
            else:
                nxt = None
            step(nxt, (j, *buf_a, tile_modes(qi, j)))
            buf_a, buf_b = buf_b, buf_a
        finalize(qi)


def _attn_call(qt, k, vt):
    bsz, seq, _ = k.shape
    tk, tq = TQ, AQ
    pairs = MLA_HEADS // 2
    n_sub = 2 * (tq // Q_SUB)
    idx = jnp.arange(tk)
    bias = jnp.where(idx[:, None] <= idx[None, :], 0.0, -jnp.inf).astype(jnp.float32)
    return pl.pallas_call(
        _attn_kernel,
        grid=(bsz, pairs),
        in_specs=[pl.BlockSpec((1, 2 * HEAD_LANES, seq), lambda b, p: (b, p, 0)),
                  pl.BlockSpec((1, seq, 2 * HEAD_LANES), lambda b, p: (b, 0, p)),
                  pl.BlockSpec((1, seq // tk, 2 * MLA_V, tk), lambda b, p: (b, 0, p, 0)),
                  _const_spec(bias.shape)],
        out_specs=pl.BlockSpec((1, seq, LANES), lambda b, p: (b, 0, p)),
        out_shape=jax.ShapeDtypeStruct((bsz, seq, MLA_WIDTH), jnp.bfloat16),
        scratch_shapes=[pltpu.VMEM((2, 1, tq), jnp.float32),
                        pltpu.VMEM((2, MLA_V + ONES_ROWS, tq), jnp.float32)]
        + [pltpu.VMEM((n_sub, tk, Q_SUB), jnp.float32)] * 2
        + [pltpu.VMEM((n_sub, 1, Q_SUB), jnp.float32)] * 2,
        compiler_params=pltpu.CompilerParams(
            dimension_semantics=("arbitrary", "arbitrary"),
            vmem_limit_bytes=VMEM_LIMIT),
        name="mla_attn",
    )(qt, k, vt, bias)


def _ret_kernel(q_ref, k_ref, v_ref, g_ref, hmask_ref, dec_ref, wq_ref, wk_ref, cd_ref, gain_ref,
                o_ref, state_ref):
    @pl.when(pl.program_id(1) == 0)
    def _():
        state_ref[...] = jnp.zeros(state_ref.shape, jnp.float32)

    c = RET_C
    for ci in range(q_ref.shape[1] // c):
        rows = slice(ci * c, (ci + 1) * c)
        q = q_ref[0, rows, :]
        k = k_ref[0, rows, :]
        for hd in range(RET_HEADS):
            vh = v_ref[0, rows, hd * RET_V:(hd + 1) * RET_V]
            qm = q * hmask_ref[hd]
            sc = _dot_nt(qm, k) * dec_ref[hd]
            inner = _dot(_bf16(sc), vh)
            st = state_ref[hd]
            cross = _dot(qm, _bf16(st)) * wq_ref[hd]
            u = _dot_tn(k, _bf16(vh.astype(jnp.float32) * wk_ref[hd]))
            state_ref[hd] = st * cd_ref[hd] + u
            o = inner + cross
            mu = jnp.mean(o, axis=-1, keepdims=True)
            var = jnp.mean(jnp.square(o - mu), axis=-1, keepdims=True)
            cols = slice(hd * RET_V, (hd + 1) * RET_V)
            on = (o - mu) * lax.rsqrt(var + EPS) * gain_ref[:, cols]
            g = g_ref[0, rows, cols].astype(jnp.float32)
            o_ref[0, rows, cols] = _bf16(g * jax.nn.sigmoid(g) * on)


def _ret_tables():
    c = RET_C
    f32 = jnp.float32
    log_gamma = jnp.log(1.0 - 2.0 ** (-5.0 - jnp.arange(RET_HEADS, dtype=f32)))
    idx = jnp.arange(c)
    rel = idx[:, None] - idx[None, :]
    k_scale = RET_QK ** -0.5
    dec = jnp.where(rel >= 0, jnp.exp(log_gamma[:, None, None] * jnp.maximum(rel, 0).astype(f32)), 0.0)
    dec = dec * k_scale
    w_q = jnp.exp(log_gamma[:, None] * (idx + 1).astype(f32))[:, :, None]
    w_k = (jnp.exp(log_gamma[:, None] * (c - 1 - idx).astype(f32)) * k_scale)[:, :, None]
    cd = jnp.exp(log_gamma * c)[:, None, None]
    w_q = jnp.broadcast_to(w_q, (RET_HEADS, c, RET_V))
    w_k = jnp.broadcast_to(w_k, (RET_HEADS, c, RET_V))
    cd = jnp.broadcast_to(cd, (RET_HEADS, 1, LANES))
    lane = jnp.arange(2 * LANES)
    hmask = ((lane % LANES) // RET_HALF)[None, :] == jnp.arange(RET_HEADS)[:, None]
    hmask = jnp.broadcast_to(hmask[:, None, :], (RET_HEADS, c, 2 * LANES)).astype(jnp.bfloat16)
    return hmask, dec, w_q, w_k, cd


def _ret_call(rq, rk, rv, rg, gn_gain):
    bsz, seq, _ = rq.shape
    tt = RET_T
    consts = _ret_tables() + (gn_gain,)
    tok = lambda n: pl.BlockSpec((1, tt, n), lambda b, i: (b, i, 0))
    return pl.pallas_call(
        _ret_kernel,
        grid=(bsz, seq // tt),
        in_specs=[tok(2 * LANES), tok(2 * LANES), tok(RET_WIDTH), tok(RET_WIDTH)]
        + [_const_spec(a.shape) for a in consts],
        out_specs=tok(RET_WIDTH),
        out_shape=jax.ShapeDtypeStruct((bsz, seq, RET_WIDTH), jnp.bfloat16),
        scratch_shapes=[pltpu.VMEM((RET_HEADS, 2 * LANES, RET_V), jnp.float32)],
        compiler_params=pltpu.CompilerParams(dimension_semantics=("arbitrary", "arbitrary"),
                                             vmem_limit_bytes=VMEM_LIMIT),
        name="retention",
    )(rq, rk, rv, rg, *consts)


def _outffn_kernel(x_ref, ym_ref, yr_ref, g1_ref, sh2_ref, sc2_ref, g2_ref,
                   gmla_ref, gpost_ref, gpre2_ref, gpost2_ref,
                   wout_ref, wg_ref, wu_ref, wd_ref, o_ref, h_ref, a_ref):
    groups = [slice(r, r + FF_ROWS) for r in range(0, x_ref.shape[0], FF_ROWS)]
    for rows in groups:
        ymn = _bf16(_rms(ym_ref[rows, :].astype(jnp.float32), gmla_ref[...]))
        mix = _dot(ymn, wout_ref[:MLA_WIDTH, :]) + _dot(yr_ref[rows, :], wout_ref[MLA_WIDTH:, :])
        x1 = x_ref[rows, :] + g1_ref[0] * _rms(mix, gpost_ref[...])
        o_ref[rows, :] = x1
        h_ref[rows, :] = _bf16(_rms(x1, gpre2_ref[...]) * (1.0 + sc2_ref[0]) + sh2_ref[0])
    for rows in groups:
        for ci in range(wg_ref.shape[1] // FF_CHUNK):
            cols = slice(ci * FF_CHUNK, (ci + 1) * FF_CHUNK)
            h = h_ref[rows, :]
            g = _dot(h, wg_ref[:, cols])
            u = _dot(h, wu_ref[:, cols])
            a_ref[rows, cols] = _bf16(g * jax.nn.sigmoid(g) * u)
    for rows in groups:
        f = _dot(a_ref[rows, :], wd_ref[...])
        o_ref[rows, :] = o_ref[rows, :] + g2_ref[0] * _rms(f, gpost2_ref[...])


def _outffn_call(x2, ym, yr, mod3, gmla, gpost, gpre2, gpost2, wout, wg3, wu3, wd, seq):
    t, d = x2.shape
    tm = TM_OUT
    per_b = seq // tm
    d_ff = wd.shape[0]
    tok = lambda n: pl.BlockSpec((tm, n), lambda i: (i, 0))
    mod_spec = lambda k: pl.BlockSpec((1, 1, d), lambda i: ((i // per_b) * 6 + k, 0, 0))
    return pl.pallas_call(
        _outffn_kernel,
        grid=(t // tm,),
        in_specs=[tok(d), tok(MLA_WIDTH), tok(RET_WIDTH),
                  mod_spec(2), mod_spec(3), mod_spec(4), mod_spec(5),
                  _const_spec(gmla.shape), _const_spec(gpost.shape), _const_spec(gpre2.shape),
                  _const_spec(gpost2.shape), _const_spec(wout.shape), _const_spec(wg3.shape),
                  _const_spec(wu3.shape), _const_spec(wd.shape)],
        out_specs=tok(d),
        out_shape=jax.ShapeDtypeStruct((t, d), jnp.float32),
        scratch_shapes=[pltpu.VMEM((tm, d), jnp.bfloat16), pltpu.VMEM((tm, d_ff), jnp.bfloat16)],
        compiler_params=pltpu.CompilerParams(dimension_semantics=("arbitrary",),
                                             vmem_limit_bytes=VMEM_LIMIT),
        name="outproj_ffn",
    )(x2, ym, yr, mod3, mod3, mod3, mod3, gmla, gpost, gpre2, gpost2, wout, wg3, wu3, wd)


def _relayout_w_in(w_in):
    d = w_in.shape[0]
    o = np.cumsum([0, MLA_Q_RANK, MLA_KV_RANK, MLA_ROPE, RET_HEADS * RET_QK, RET_HEADS * RET_QK,
                   RET_WIDTH, RET_WIDTH])
    cq, ckv, kpe, rq, rk, rv, rg = [w_in[:, o[i]:o[i + 1]] for i in range(7)]
    kpe = jnp.concatenate([jnp.zeros((d, MLA_NOPE), w_in.dtype), kpe,
                           jnp.zeros((d, HEAD_LANES - MLA_QK), w_in.dtype)], axis=1)

    def halves_first(w):
        return w.reshape(d, RET_HEADS, 2, RET_HALF).transpose(0, 2, 1, 3).reshape(d, RET_HEADS * RET_QK)

    return _bf16(jnp.concatenate([cq, ckv, kpe, halves_first(rq), halves_first(rk), rv, rg], axis=1))


def _relayout_w_q_b(w):
    r = w.shape[0]
    w = w.reshape(r, MLA_HEADS, MLA_QK)
    w = jnp.pad(w, ((0, 0), (0, 0), (0, HEAD_LANES - MLA_QK)))
    return _bf16(w.reshape(r, MLA_HEADS * HEAD_LANES).T)


def _relayout_w_kv_b(w):
    r = w.shape[0]
    w = w.reshape(r, MLA_HEADS, MLA_NOPE + MLA_V)
    wk = jnp.pad(w[..., :MLA_NOPE], ((0, 0), (0, 0), (0, HEAD_LANES - MLA_NOPE)))
    wv = w[..., MLA_NOPE:]
    return _bf16(wk.reshape(r, MLA_HEADS * HEAD_LANES)), _bf16(wv.reshape(r, MLA_WIDTH).T)


def _rope_inv_col():
    inv_m = ROPE_BASE ** (-jnp.arange(0, MLA_ROPE, 2, dtype=jnp.float32) / MLA_ROPE)
    inv_r = ROPE_BASE ** (-jnp.arange(0, RET_QK, 2, dtype=jnp.float32) / RET_QK)
    return jnp.concatenate([inv_m, inv_r]).reshape(N_FREQ, 1)


def kernel(x, c, positions, w_ada, b_ada, pre_norm_mix, w_in, q_a_norm, w_q_b, kv_a_norm, w_kv_b,
           mla_out_norm, ret_gn_gain, w_out, post_norm_mix, pre_norm_ffn, w_gate, w_up, w_down,
           post_norm_ffn):
    bsz, seq, d = x.shape
    t = bsz * seq
    depth = w_ada.shape[0]
    d_ff = w_gate.shape[-1]
    assert seq % TM_IN == 0 and seq % AQ == 0 and seq % RET_T == 0 and seq % TM_OUT == 0
    assert AQ % TQ == 0 and TQ % Q_SUB == 0
    assert RET_T % RET_C == 0 and d_ff % FF_CHUNK == 0
    assert IN_ROWS % TQ == 0 and TM_IN % IN_ROWS == 0 and TM_OUT % FF_ROWS == 0

    row = lambda g: g.reshape(1, -1)
    pos3 = positions.reshape(t // TM_IN, 1, TM_IN)
    inv_col = _rope_inv_col()
    x2 = x.reshape(t, d)
    for l in range(depth):
        mod3 = _mod_call(c, w_ada[l], b_ada[l]).reshape(bsz * 6, 1, d)
        wk_p, wv_p = _relayout_w_kv_b(w_kv_b[l])
        q, k, v, rq, rk, rv, rg = _inproj_call(
            x2, mod3, pos3, inv_col, row(pre_norm_mix[l]), _relayout_w_in(w_in[l]),
            row(q_a_norm[l]), _relayout_w_q_b(w_q_b[l]), row(kv_a_norm[l]), wk_p, wv_p, seq)
        b3 = lambda a: a.reshape(bsz, seq, a.shape[-1])
        y_mla = _attn_call(q, b3(k), v)
        y_ret = _ret_call(b3(rq), b3(rk), b3(rv), b3(rg), row(ret_gn_gain[l]))
        x2 = _outffn_call(
            x2, y_mla.reshape(t, MLA_WIDTH), y_ret.reshape(t, RET_WIDTH), mod3,
            row(mla_out_norm[l]), row(post_norm_mix[l]), row(pre_norm_ffn[l]), row(post_norm_ffn[l]),
            _bf16(w_out[l]), _bf16(w_gate[l]), _bf16(w_up[l]), _bf16(w_down[l]), seq)
    return x2.reshape(bsz, seq, d)
```

```python
import functools

import jax
import jax.numpy as jnp
import numpy as np
from jax import lax
from jax.experimental import pallas as pl
from jax.experimental.pallas import tpu as pltpu

MLA_HEADS = 8
MLA_NOPE = 64
MLA_ROPE = 32
MLA_V = 64
MLA_Q_RANK = 384
MLA_KV_RANK = 256
MLA_QK = MLA_NOPE + MLA_ROPE
RET_HEADS = 4
RET_QK = 64
RET_V = 128
MLA_WIDTH = MLA_HEADS * MLA_V
RET_WIDTH = RET_HEADS * RET_V
ROPE_BASE = 10000.0
EPS = 1e-6
LOG2E = 1.4426950408889634

LANES = 128
VMEM_LIMIT = 56 * 1024 * 1024

TM_IN = 1024
IN_ROWS = 1024
TQ = 512
AQ = 512
RET_C = 256
RET_T = 2048
TM_OUT = 1024
FF_CHUNK = 256
FF_ROWS = 512

HEAD_LANES = LANES
ROPE_HALF = MLA_ROPE // 2
RET_HALF = RET_QK // 2
N_FREQ = ROPE_HALF + RET_HALF


def _bf16(x):
    return x.astype(jnp.bfloat16)


def _dot(a, b):
    return jnp.dot(a, b, preferred_element_type=jnp.float32)


def _dot_nt(a, b):
    return lax.dot_general(a, b, (((1,), (1,)), ((), ())), preferred_element_type=jnp.float32)


def _dot_tn(a, b):
    return lax.dot_general(a, b, (((0,), (0,)), ((), ())), preferred_element_type=jnp.float32)


def _rms(x, gain):
    return x * lax.rsqrt(jnp.mean(x * x, axis=-1, keepdims=True) + EPS) * gain


def _const_spec(shape):
    nd = len(shape)
    return pl.BlockSpec(shape, lambda *_: (0,) * nd, pipeline_mode=pl.Buffered(1))


def _mod_kernel(c_ref, w_ref, b_ref, o_ref):
    c = c_ref[...]
    a = _bf16(c * jax.nn.sigmoid(c))
    o_ref[...] = _dot(a, _bf16(w_ref[...])) + b_ref[...]


def _mod_call(c, w_ada, b_ada):
    bsz, d = c.shape
    n = w_ada.shape[1]
    tn = 1024
    return pl.pallas_call(
        _mod_kernel,
        grid=(n // tn,),
        in_specs=[pl.BlockSpec((bsz, d), lambda j: (0, 0)),
                  pl.BlockSpec((d, tn), lambda j: (0, j)),
                  pl.BlockSpec((1, tn), lambda j: (0, j))],
        out_specs=pl.BlockSpec((bsz, tn), lambda j: (0, j)),
        out_shape=jax.ShapeDtypeStruct((bsz, n), jnp.float32),
        compiler_params=pltpu.CompilerParams(dimension_semantics=("arbitrary",),
                                             vmem_limit_bytes=VMEM_LIMIT),
        name="adaln_mod",
    )(c, w_ada, b_ada.reshape(1, n))


_O_CQ = 0
_O_CKV = _O_CQ + MLA_Q_RANK
_O_KPE = _O_CKV + MLA_KV_RANK
_O_RQ = _O_KPE + HEAD_LANES
_O_RK = _O_RQ + RET_HEADS * RET_QK
_O_RV = _O_RK + RET_HEADS * RET_QK
_O_RG = _O_RV + RET_WIDTH
_IN_COLS_P = _O_RG + RET_WIDTH


def _swap_rope_halves(x):
    lane = lax.broadcasted_iota(jnp.int32, x.shape, 1)
    from_right = pltpu.roll(x, HEAD_LANES - ROPE_HALF, 1)
    from_left = pltpu.roll(x, ROPE_HALF, 1)
    return jnp.where(lane < MLA_NOPE + ROPE_HALF, from_right, from_left)


def _inproj_kernel(x_ref, sc_ref, sh_ref, pos_ref, inv_ref, gpre_ref, win_ref, gq_ref, wqbt_ref,
                   gkv_ref, wk_ref, wvt_ref,
                   qt_ref, k_ref, vt_ref, rq_ref, rk_ref, rv_ref, rg_ref):
    for g in range(x_ref.shape[0] // IN_ROWS):
        _inproj_rows(g, x_ref, sc_ref, sh_ref, pos_ref, inv_ref, gpre_ref, win_ref, gq_ref, wqbt_ref,
                     gkv_ref, wk_ref, wvt_ref, qt_ref, k_ref, vt_ref, rq_ref, rk_ref, rv_ref, rg_ref)


def _inproj_rows(g, x_ref, sc_ref, sh_ref, pos_ref, inv_ref, gpre_ref, win_ref, gq_ref, wqbt_ref,
                 gkv_ref, wk_ref, wvt_ref, qt_ref, k_ref, vt_ref, rq_ref, rk_ref, rv_ref, rg_ref):
    tm = IN_ROWS
    rows = slice(g * tm, (g + 1) * tm)
    x = x_ref[rows, :]
    h = _bf16(_rms(x, gpre_ref[...]) * (1.0 + sc_ref[0]) + sh_ref[0])

    pos = pos_ref[0, :, rows].astype(jnp.float32)
    ang = inv_ref[...] * pos
    cs = jnp.cos(ang)
    sn = jnp.sin(ang)
    c_m, c_r = cs[:ROPE_HALF], cs[ROPE_HALF:]
    s_m, s_r = sn[:ROPE_HALF], sn[ROPE_HALF:]
    ones = jnp.ones((MLA_NOPE, tm), jnp.float32)
    zeros_hi = jnp.zeros((HEAD_LANES - MLA_QK, tm), jnp.float32)
    zeros_lo = jnp.zeros((MLA_NOPE, tm), jnp.float32)
    cos_k = jnp.concatenate([ones, c_m, c_m, zeros_hi], axis=0).T
    sin_k = jnp.concatenate([zeros_lo, -s_m, s_m, zeros_hi], axis=0).T
    cos_r = jnp.concatenate([c_r] * RET_HEADS, axis=0).T
    sin_r = jnp.concatenate([s_r] * RET_HEADS, axis=0).T
    scale = MLA_QK ** -0.5 * LOG2E

    cq = _dot(h, win_ref[:, _O_CQ:_O_CQ + MLA_Q_RANK])
    ckv = _dot(h, win_ref[:, _O_CKV:_O_CKV + MLA_KV_RANK])
    kpe = _dot(h, win_ref[:, _O_KPE:_O_KPE + HEAD_LANES])
    rv_ref[rows, :] = _bf16(_dot(h, win_ref[:, _O_RV:_O_RV + RET_WIDTH]))
    rg_ref[rows, :] = _bf16(_dot(h, win_ref[:, _O_RG:_O_RG + RET_WIDTH]))
    qt = _dot_nt(wqbt_ref[...], _bf16(_rms(cq, gq_ref[...])))
    cos_q, sin_q = c_m * scale, s_m * scale
    for hd in range(MLA_HEADS):
        base = hd * HEAD_LANES
        x1 = qt[base + MLA_NOPE:base + MLA_NOPE + ROPE_HALF]
        x2 = qt[base + MLA_NOPE + ROPE_HALF:base + MLA_QK]
        head = jnp.concatenate([qt[base:base + MLA_NOPE] * scale,
                                x1 * cos_q - x2 * sin_q, x1 * sin_q + x2 * cos_q,
                                qt[base + MLA_QK:base + HEAD_LANES]], axis=0)
        qt_ref[0, base:base + HEAD_LANES, rows] = _bf16(head)

    ckvn = _bf16(_rms(ckv, gkv_ref[...]))
    kpe = kpe * cos_k + _swap_rope_halves(kpe) * sin_k
    kn = _dot(ckvn, wk_ref[...])
    for hd in range(MLA_HEADS):
        sl = slice(hd * HEAD_LANES, (hd + 1) * HEAD_LANES)
        k_ref[rows, sl] = _bf16(kn[:, sl] + kpe)
    vt = _bf16(_dot_nt(wvt_ref[...], ckvn))
    per_group = tm // TQ
    for si in range(per_group):
        vt_ref[0, g * per_group + si] = vt[:, si * TQ:(si + 1) * TQ]

    for off, o_ref in ((_O_RQ, rq_ref), (_O_RK, rk_ref)):
        z = _dot(h, win_ref[:, off:off + 2 * LANES])
        z1, z2 = z[:, :LANES], z[:, LANES:]
        o_ref[rows, :LANES] = _bf16(z1 * cos_r - z2 * sin_r)
        o_ref[rows, LANES:] = _bf16(z1 * sin_r + z2 * cos_r)


def _inproj_call(x2, mod3, pos3, inv_col, gpre, win_p, gq, wqb_p, gkv, wk_p, wv_p, seq):
    t, d = x2.shape
    tm = TM_IN
    per_b = seq // tm
    tok = lambda n: pl.BlockSpec((tm, n), lambda i: (i, 0))
    mod_spec = lambda k: pl.BlockSpec((1, 1, d), lambda i: ((i // per_b) * 6 + k, 0, 0))
    tok_widths = [MLA_HEADS * HEAD_LANES, RET_HEADS * RET_QK, RET_HEADS * RET_QK, RET_WIDTH, RET_WIDTH]
    qt_spec = pl.BlockSpec((1, MLA_HEADS * HEAD_LANES, tm), lambda i: (i // per_b, 0, i % per_b))
    qt_shape = jax.ShapeDtypeStruct((t // seq, MLA_HEADS * HEAD_LANES, seq), jnp.bfloat16)
    slabs = tm // TQ
    vt_spec = pl.BlockSpec((1, slabs, MLA_WIDTH, TQ), lambda i: (i // per_b, i % per_b, 0, 0))
    vt_shape = jax.ShapeDtypeStruct((t // seq, seq // TQ, MLA_WIDTH, TQ), jnp.bfloat16)
    tok_specs = [tok(n) for n in tok_widths]
    tok_shapes = [jax.ShapeDtypeStruct((t, n), jnp.bfloat16) for n in tok_widths]
    return pl.pallas_call(
        _inproj_kernel,
        grid=(t // tm,),
        in_specs=[tok(d), mod_spec(1), mod_spec(0),
                  pl.BlockSpec((1, 1, tm), lambda i: (i, 0, 0)),
                  _const_spec(inv_col.shape), _const_spec(gpre.shape), _const_spec(win_p.shape),
                  _const_spec(gq.shape), _const_spec(wqb_p.shape), _const_spec(gkv.shape),
                  _const_spec(wk_p.shape), _const_spec(wv_p.shape)],
        out_specs=[qt_spec, tok_specs[0], vt_spec] + tok_specs[1:],
        out_shape=[qt_shape, tok_shapes[0], vt_shape] + tok_shapes[1:],
        compiler_params=pltpu.CompilerParams(dimension_semantics=("arbitrary",),
                                             vmem_limit_bytes=VMEM_LIMIT),
        name="inproj",
    )(x2, mod3, mod3, pos3, inv_col, gpre, win_p, gq, wqb_p, gkv, wk_p, wv_p)


Q_SUB = 256
PV_LAG = 2
ONES_ROWS = 16


def _attn_kernel(qt_ref, k_ref, vt_ref, bias_ref, o_ref, m_ref, acc_ref, sa_ref, sb_ref, mxa_ref, mxb_ref):
    tk, tq = TQ, AQ
    r_diag = tq // tk
    n_q = qt_ref.shape[2] // tq
    ones = jnp.ones((ONES_ROWS, tk), jnp.bfloat16)
    subs = [(hd, qo) for hd in range(2) for qo in range(0, tq, Q_SUB)]
    full = ["full"] * len(subs)

    def diag_modes(r):
        modes = []
        for _, qo in subs:
            d = qo - r * tk
            modes.append(None if d + Q_SUB <= 0 else ("full" if d >= tk else d))
        return modes

    def key_extent(mode):
        return tk if mode == "full" else min(tk, mode + Q_SUB)

    def produce(qi, j, s_ref, mx_ref, modes, only=None):
        start = j * tk if isinstance(j, int) else pl.multiple_of(j * tk, tk)
        for i, (hd, qo) in enumerate(subs):
            if modes[i] is None or (only is not None and i != only):
                continue
            sl = slice(hd * HEAD_LANES, (hd + 1) * HEAD_LANES)
            qt = qt_ref[0, sl, qi * tq + qo:qi * tq + qo + Q_SUB]
            kx = key_extent(modes[i])
            st = _dot(k_ref[0, pl.ds(start, kx), sl], qt)
            if modes[i] != "full":
                st = st + bias_ref[:kx, modes[i]:modes[i] + Q_SUB]
            s_ref[i, :kx] = st
            mx_ref[i] = jnp.max(st, axis=0, keepdims=True)

    def step(prod, cons):
        for i in range(len(subs) + PV_LAG):
            if prod is not None and i < len(subs):
                produce(*prod, only=i)
            if cons is not None and i >= PV_LAG:
                consume(*cons, only=i - PV_LAG)

    def consume(j, s_ref, mx_ref, modes, only=None):
        for i, (hd, qo) in enumerate(subs):
            if modes[i] is None or (only is not None and i != only):
                continue
            qs = slice(qo, qo + Q_SUB)
            kx = key_extent(modes[i])
            vt = vt_ref[0, j, hd * MLA_V:(hd + 1) * MLA_V, :kx]
            lhs = jnp.concatenate([vt, ones[:, :kx]], axis=0)
            st = s_ref[i, :kx]
            m_prev = m_ref[hd, :, qs]
            m_new = jnp.maximum(m_prev, mx_ref[i])
            alpha = jnp.exp2(m_prev - m_new)
            p = _bf16(jnp.exp2(st - m_new))
            acc_ref[hd, :, qs] = alpha * acc_ref[hd, :, qs] + _dot(lhs, p)
            m_ref[hd, :, qs] = m_new

    def finalize(qi):
        outs = []
        for hd in range(2):
            acc = acc_ref[hd]
            outs.append(acc[:MLA_V] / acc[MLA_V:MLA_V + 1])
        o_ref[0, qi * tq:(qi + 1) * tq, :] = _bf16(jnp.concatenate(outs, axis=0).T)

    def tile_modes(qi, j):
        n_full = r_diag * qi
        return full if j < n_full else diag_modes(j - n_full)

    buf_a, buf_b = (sa_ref, mxa_ref), (sb_ref, mxb_ref)
    produce(0, 0, *buf_a, tile_modes(0, 0))
    for qi in range(n_q):
        m_ref[...] = jnp.full(m_ref.shape, -jnp.inf, jnp.float32)
        acc_ref[...] = jnp.zeros(acc_ref.shape, jnp.float32)
        n_full = r_diag * qi
        last = n_full + r_diag - 1
        n_pairs = max(0, (n_full - 1) // 2)

        def pair_body(i, carry, qi=qi, buf_a=buf_a, buf_b=buf_b):
            j = 2 * i
            step((qi, j + 1, *buf_b, full), (j, *buf_a, full))
            step((qi, j + 2, *buf_a, full), (j + 1, *buf_b, full))
            return carry

        if n_pairs > 0:
            lax.fori_loop(0, n_pairs, pair_body, 0)
        for j in range(2 * n_pairs, last + 1):
            if j < last:
                nxt = (qi, j + 1, *buf_b, tile_modes(qi, j + 1))
            elif qi + 1 < n_q:
                nxt = (qi + 1, 0, *buf_b, tile_modes(qi + 1, 0))
            else:
                nxt = None
            step(nxt, (j, *buf_a, tile_modes(qi, j)))
            buf_a, buf_b = buf_b, buf_a
        finalize(qi)


def _attn_call(qt, k, vt):
    bsz, seq, _ = k.shape
    tk, tq = TQ, AQ
    pairs = MLA_HEADS // 2
    n_sub = 2 * (tq // Q_SUB)
    idx = jnp.arange(tk)
    bias = jnp.where(idx[:, None] <= idx[None, :], 0.0, -jnp.inf).astype(jnp.float32)
    return pl.pallas_call(
        _attn_kernel,
        grid=(bsz, pairs),
        in_specs=[pl.BlockSpec((1, 2 * HEAD_LANES, seq), lambda b, p: (b, p, 0)),
                  pl.BlockSpec((1, seq, 2 * HEAD_LANES), lambda b, p: (b, 0, p)),
                  pl.BlockSpec((1, seq // tk, 2 * MLA_V, tk), lambda b, p: (b, 0, p, 0)),
                  _const_spec(bias.shape)],
        out_specs=pl.BlockSpec((1, seq, LANES), lambda b, p: (b, 0, p)),
        out_shape=jax.ShapeDtypeStruct((bsz, seq, MLA_WIDTH), jnp.bfloat16),
        scratch_shapes=[pltpu.VMEM((2, 1, tq), jnp.float32),
                        pltpu.VMEM((2, MLA_V + ONES_ROWS, tq), jnp.float32)]
        + [pltpu.VMEM((n_sub, tk, Q_SUB), jnp.float32)] * 2
        + [pltpu.VMEM((n_sub, 1, Q_SUB), jnp.float32)] * 2,
        compiler_params=pltpu.CompilerParams(
            dimension_semantics=("arbitrary", "arbitrary"),
            vmem_limit_bytes=VMEM_LIMIT),
        name="mla_attn",
    )(qt, k, vt, bias)


def _ret_kernel(q_ref, k_ref, v_ref, g_ref, hmask_ref, dec_ref, wq_ref, wk_ref, cd_ref, gain_ref,
                o_ref, state_ref):
    @pl.when(pl.program_id(1) == 0)
    def _():
        state_ref[...] = jnp.zeros(state_ref.shape, jnp.float32)

    c = RET_C
    for ci in range(q_ref.shape[1] // c):
        rows = slice(ci * c, (ci + 1) * c)
        q = q_ref[0, rows, :]
        k = k_ref[0, rows, :]
        for hd in range(RET_HEADS):
            vh = v_ref[0, rows, hd * RET_V:(hd + 1) * RET_V]
            qm = q * hmask_ref[hd]
            sc = _dot_nt(qm, k) * dec_ref[hd]
            inner = _dot(_bf16(sc), vh)
            st = state_ref[hd]
            cross = _dot(qm, _bf16(st)) * wq_ref[hd]
            u = _dot_tn(k, _bf16(vh.astype(jnp.float32) * wk_ref[hd]))
            state_ref[hd] = st * cd_ref[hd] + u
            o = inner + cross
            mu = jnp.mean(o, axis=-1, keepdims=True)
            var = jnp.mean(jnp.square(o - mu), axis=-1, keepdims=True)
            cols = slice(hd * RET_V, (hd + 1) * RET_V)
            on = (o - mu) * lax.rsqrt(var + EPS) * gain_ref[:, cols]
            g = g_ref[0, rows, cols].astype(jnp.float32)
            o_ref[0, rows, cols] = _bf16(g * jax.nn.sigmoid(g) * on)


def _ret_tables():
    c = RET_C
    f32 = jnp.float32
    log_gamma = jnp.log(1.0 - 2.0 ** (-5.0 - jnp.arange(RET_HEADS, dtype=f32)))
    idx = jnp.arange(c)
    rel = idx[:, None] - idx[None, :]
    k_scale = RET_QK ** -0.5
    dec = jnp.where(rel >= 0, jnp.exp(log_gamma[:, None, None] * jnp.maximum(rel, 0).astype(f32)), 0.0)
    dec = dec * k_scale
    w_q = jnp.exp(log_gamma[:, None] * (idx + 1).astype(f32))[:, :, None]
    w_k = (jnp.exp(log_gamma[:, None] * (c - 1 - idx).astype(f32)) * k_scale)[:, :, None]
    cd = jnp.exp(log_gamma * c)[:, None, None]
    w_q = jnp.broadcast_to(w_q, (RET_HEADS, c, RET_V))
    w_k = jnp.broadcast_to(w_k, (RET_HEADS, c, RET_V))
    cd = jnp.broadcast_to(cd, (RET_HEADS, 1, LANES))
    lane = jnp.arange(2 * LANES)
    hmask = ((lane % LANES) // RET_HALF)[None, :] == jnp.arange(RET_HEADS)[:, None]
    hmask = jnp.broadcast_to(hmask[:, None, :], (RET_HEADS, c, 2 * LANES)).astype(jnp.bfloat16)
    return hmask, dec, w_q, w_k, cd


def _ret_call(rq, rk, rv, rg, gn_gain):
    bsz, seq, _ = rq.shape
    tt = RET_T
    consts = _ret_tables() + (gn_gain,)
    tok = lambda n: pl.BlockSpec((1, tt, n), lambda b, i: (b, i, 0))
    return pl.pallas_call(
        _ret_kernel,
        grid=(bsz, seq // tt),
        in_specs=[tok(2 * LANES), tok(2 * LANES), tok(RET_WIDTH), tok(RET_WIDTH)]
        + [_const_spec(a.shape) for a in consts],
        out_specs=tok(RET_WIDTH),
        out_shape=jax.ShapeDtypeStruct((bsz, seq, RET_WIDTH), jnp.bfloat16),
        scratch_shapes=[pltpu.VMEM((RET_HEADS, 2 * LANES, RET_V), jnp.float32)],
        compiler_params=pltpu.CompilerParams(dimension_semantics=("arbitrary", "arbitrary"),
                                             vmem_limit_bytes=VMEM_LIMIT),
        name="retention",
    )(rq, rk, rv, rg, *consts)


def _outffn_kernel(x_ref, ym_ref, yr_ref, g1_ref, sh2_ref, sc2_ref, g2_ref,
                   gmla_ref, gpost_ref, gpre2_ref, gpost2_ref,
                   wout_ref, wg_ref, wu_ref, wd_ref, o_ref, h_ref, a_ref):
    groups = [slice(r, r + FF_ROWS) for r in range(0, x_ref.shape[0], FF_ROWS)]
    for rows in groups:
        ymn = _bf16(_rms(ym_ref[rows, :].astype(jnp.float32), gmla_ref[...]))
        mix = _dot(ymn, wout_ref[:MLA_WIDTH, :]) + _dot(yr_ref[rows, :], wout_ref[MLA_WIDTH:, :])
        x1 = x_ref[rows, :] + g1_ref[0] * _rms(mix, gpost_ref[...])
        o_ref[rows, :] = x1
        h_ref[rows, :] = _bf16(_rms(x1, gpre2_ref[...]) * (1.0 + sc2_ref[0]) + sh2_ref[0])
    for rows in groups:
        for ci in range(wg_ref.shape[1] // FF_CHUNK):
            cols = slice(ci * FF_CHUNK, (ci + 1) * FF_CHUNK)
            h = h_ref[rows, :]
            g = _dot(h, wg_ref[:, cols])
            u = _dot(h, wu_ref[:, cols])
            a_ref[rows, cols] = _bf16(g * jax.nn.sigmoid(g) * u)
    for rows in groups:
        f = _dot(a_ref[rows, :], wd_ref[...])
        o_ref[rows, :] = o_ref[rows, :] + g2_ref[0] * _rms(f, gpost2_ref[...])


def _outffn_call(x2, ym, yr, mod3, gmla, gpost, gpre2, gpost2, wout, wg3, wu3, wd, seq):
    t, d = x2.shape
    tm = TM_OUT
    per_b = seq // tm
    d_ff = wd.shape[0]
    tok = lambda n: pl.BlockSpec((tm, n), lambda i: (i, 0))
    mod_spec = lambda k: pl.BlockSpec((1, 1, d), lambda i: ((i // per_b) * 6 + k, 0, 0))
    return pl.pallas_call(
        _outffn_kernel,
        grid=(t // tm,),
        in_specs=[tok(d), tok(MLA_WIDTH), tok(RET_WIDTH),
                  mod_spec(2), mod_spec(3), mod_spec(4), mod_spec(5),
                  _const_spec(gmla.shape), _const_spec(gpost.shape), _const_spec(gpre2.shape),
                  _const_spec(gpost2.shape), _const_spec(wout.shape), _const_spec(wg3.shape),
                  _const_spec(wu3.shape), _const_spec(wd.shape)],
        out_specs=tok(d),
        out_shape=jax.ShapeDtypeStruct((t, d), jnp.float32),
        scratch_shapes=[pltpu.VMEM((tm, d), jnp.bfloat16), pltpu.VMEM((tm, d_ff), jnp.bfloat16)],
        compiler_params=pltpu.CompilerParams(dimension_semantics=("arbitrary",),
                                             vmem_limit_bytes=VMEM_LIMIT),
        name="outproj_ffn",
    )(x2, ym, yr, mod3, mod3, mod3, mod3, gmla, gpost, gpre2, gpost2, wout, wg3, wu3, wd)


def _relayout_w_in(w_in):
    d = w_in.shape[0]
    o = np.cumsum([0, MLA_Q_RANK, MLA_KV_RANK, MLA_ROPE, RET_HEADS * RET_QK, RET_HEADS * RET_QK,
                   RET_WIDTH, RET_WIDTH])
    cq, ckv, kpe, rq, rk, rv, rg = [w_in[:, o[i]:o[i + 1]] for i in range(7)]
    kpe = jnp.concatenate([jnp.zeros((d, MLA_NOPE), w_in.dtype), kpe,
                           jnp.zeros((d, HEAD_LANES - MLA_QK), w_in.dtype)], axis=1)

    def halves_first(w):
        return w.reshape(d, RET_HEADS, 2, RET_HALF).transpose(0, 2, 1, 3).reshape(d, RET_HEADS * RET_QK)

    return _bf16(jnp.concatenate([cq, ckv, kpe, halves_first(rq), halves_first(rk), rv, rg], axis=1))


def _relayout_w_q_b(w):
    r = w.shape[0]
    w = w.reshape(r, MLA_HEADS, MLA_QK)
    w = jnp.pad(w, ((0, 0), (0, 0), (0, HEAD_LANES - MLA_QK)))
    return _bf16(w.reshape(r, MLA_HEADS * HEAD_LANES).T)


def _relayout_w_kv_b(w):
    r = w.shape[0]
    w = w.reshape(r, MLA_HEADS, MLA_NOPE + MLA_V)
    wk = jnp.pad(w[..., :MLA_NOPE], ((0, 0), (0, 0), (0, HEAD_LANES - MLA_NOPE)))
    wv = w[..., MLA_NOPE:]
    return _bf16(wk.reshape(r, MLA_HEADS * HEAD_LANES)), _bf16(wv.reshape(r, MLA_WIDTH).T)


def _rope_inv_col():
    inv_m = ROPE_BASE ** (-jnp.arange(0, MLA_ROPE, 2, dtype=jnp.float32) / MLA_ROPE)
    inv_r = ROPE_BASE ** (-jnp.arange(0, RET_QK, 2, dtype=jnp.float32) / RET_QK)
    return jnp.concatenate([inv_m, inv_r]).reshape(N_FREQ, 1)


def kernel(x, c, positions, w_ada, b_ada, pre_norm_mix, w_in, q_a_norm, w_q_b, kv_a_norm, w_kv_b,
           mla_out_norm, ret_gn_gain, w_out, post_norm_mix, pre_norm_ffn, w_gate, w_up, w_down,
           post_norm_ffn):
    bsz, seq, d = x.shape
    t = bsz * seq
    depth = w_ada.shape[0]
    d_ff = w_gate.shape[-1]
    assert seq % TM_IN == 0 and seq % AQ == 0 and seq % RET_T == 0 and seq % TM_OUT == 0
    assert AQ % TQ == 0 and TQ % Q_SUB == 0
    assert RET_T % RET_C == 0 and d_ff % FF_CHUNK == 0
    assert IN_ROWS % TQ == 0 and TM_IN % IN_ROWS == 0 and TM_OUT % FF_ROWS == 0

    row = lambda g: g.reshape(1, -1)
    pos3 = positions.reshape(t // TM_IN, 1, TM_IN)
    inv_col = _rope_inv_col()
    x2 = x.reshape(t, d)
    for l in range(depth):
        mod3 = _mod_call(c, w_ada[l], b_ada[l]).reshape(bsz * 6, 1, d)
        wk_p, wv_p = _relayout_w_kv_b(w_kv_b[l])
        q, k, v, rq, rk, rv, rg = _inproj_call(
            x2, mod3, pos3, inv_col, row(pre_norm_mix[l]), _relayout_w_in(w_in[l]),
            row(q_a_norm[l]), _relayout_w_q_b(w_q_b[l]), row(kv_a_norm[l]), wk_p, wv_p, seq)
        b3 = lambda a: a.reshape(bsz, seq, a.shape[-1])
        y_mla = _attn_call(q, b3(k), v)
        y_ret = _ret_call(b3(rq), b3(rk), b3(rv), b3(rg), row(ret_gn_gain[l]))
        x2 = _outffn_call(
            x2, y_mla.reshape(t, MLA_WIDTH), y_ret.reshape(t, RET_WIDTH), mod3,
            row(mla_out_norm[l]), row(post_norm_mix[l]), row(pre_norm_ffn[l]), row(post_norm_ffn[l]),
            _bf16(w_out[l]), _bf16(w_gate[l]), _bf16(w_up[l]), _bf16(w_down[l]), seq)
    return x2.reshape(bsz, seq, d)
```

```python
import functools

import jax
import jax.numpy as jnp
import numpy as np
from jax import lax
from jax.experimental import pallas as pl
from jax.experimental.pallas import tpu as pltpu

MLA_HEADS = 8
MLA_NOPE = 64
MLA_ROPE = 32
MLA_V = 64
MLA_Q_RANK = 384
MLA_KV_RANK = 256
MLA_QK = MLA_NOPE + MLA_ROPE
RET_HEADS = 4
RET_QK = 64
RET_V = 128
MLA_WIDTH = MLA_HEADS * MLA_V
RET_WIDTH = RET_HEADS * RET_V
ROPE_BASE = 10000.0
EPS = 1e-6
LOG2E = 1.4426950408889634

LANES = 128
VMEM_LIMIT = 56 * 1024 * 1024

TM_IN = 1024
IN_ROWS = 1024
TQ = 512
AQ = 512
RET_C = 256
RET_T = 2048
TM_OUT = 1024
FF_CHUNK = 256
FF_ROWS = 512

HEAD_LANES = LANES
ROPE_HALF = MLA_ROPE // 2
RET_HALF = RET_QK // 2
N_FREQ = ROPE_HALF + RET_HALF


def _bf16(x):
    return x.astype(jnp.bfloat16)


def _dot(a, b):
    return jnp.dot(a, b, preferred_element_type=jnp.float32)


def _dot_nt(a, b):
    return lax.dot_general(a, b, (((1,), (1,)), ((), ())), preferred_element_type=jnp.float32)


def _dot_tn(a, b):
    return lax.dot_general(a, b, (((0,), (0,)), ((), ())), preferred_element_type=jnp.float32)


def _rms(x, gain):
    return x * lax.rsqrt(jnp.mean(x * x, axis=-1, keepdims=True) + EPS) * gain


def _const_spec(shape):
    nd = len(shape)
    return pl.BlockSpec(shape, lambda *_: (0,) * nd, pipeline_mode=pl.Buffered(1))


def _mod_kernel(c_ref, w_ref, b_ref, o_ref):
    c = c_ref[...]
    a = _bf16(c * jax.nn.sigmoid(c))
    o_ref[...] = _dot(a, _bf16(w_ref[...])) + b_ref[...]


def _mod_call(c, w_ada, b_ada):
    bsz, d = c.shape
    n = w_ada.shape[1]
    tn = 1024
    return pl.pallas_call(
        _mod_kernel,
        grid=(n // tn,),
        in_specs=[pl.BlockSpec((bsz, d), lambda j: (0, 0)),
                  pl.BlockSpec((d, tn), lambda j: (0, j)),
                  pl.BlockSpec((1, tn), lambda j: (0, j))],
        out_specs=pl.BlockSpec((bsz, tn), lambda j: (0, j)),
        out_shape=jax.ShapeDtypeStruct((bsz, n), jnp.float32),
        compiler_params=pltpu.CompilerParams(dimension_semantics=("arbitrary",),
                                             vmem_limit_bytes=VMEM_LIMIT),
        name="adaln_mod",
    )(c, w_ada, b_ada.reshape(1, n))


_O_CQ = 0
_O_KPE = _O_CQ + MLA_Q_RANK
_O_CKV = _O_KPE + HEAD_LANES
_O_RQ = _O_CKV + MLA_KV_RANK
_O_RK = _O_RQ + RET_HEADS * RET_QK
_O_RV = _O_RK + RET_HEADS * RET_QK
_O_RG = _O_RV + RET_WIDTH
_IN_COLS_P = _O_RG + RET_WIDTH


def _swap_rope_halves(x):
    lane = lax.broadcasted_iota(jnp.int32, x.shape, 1)
    from_right = pltpu.roll(x, HEAD_LANES - ROPE_HALF, 1)
    from_left = pltpu.roll(x, ROPE_HALF, 1)
    return jnp.where(lane < MLA_NOPE + ROPE_HALF, from_right, from_left)


def _inproj_kernel(x_ref, sc_ref, sh_ref, pos_ref, inv_ref, gpre_ref, win_ref, gq_ref, wqbt_ref,
                   gkv_ref, wk_ref, wvt_ref,
                   qt_ref, k_ref, vt_ref, rq_ref, rk_ref, rv_ref, rg_ref):
    for g in range(x_ref.shape[0] // IN_ROWS):
        _inproj_rows(g, x_ref, sc_ref, sh_ref, pos_ref, inv_ref, gpre_ref, win_ref, gq_ref, wqbt_ref,
                     gkv_ref, wk_ref, wvt_ref, qt_ref, k_ref, vt_ref, rq_ref, rk_ref, rv_ref, rg_ref)


def _inproj_rows(g, x_ref, sc_ref, sh_ref, pos_ref, inv_ref, gpre_ref, win_ref, gq_ref, wqbt_ref,
                 gkv_ref, wk_ref, wvt_ref, qt_ref, k_ref, vt_ref, rq_ref, rk_ref, rv_ref, rg_ref):
    tm = IN_ROWS
    rows = slice(g * tm, (g + 1) * tm)
    x = x_ref[rows, :]
    h = _bf16(_rms(x, gpre_ref[...]) * (1.0 + sc_ref[0]) + sh_ref[0])

    pos = pos_ref[0, :, rows].astype(jnp.float32)
    ang = inv_ref[...] * pos
    cs = jnp.cos(ang)
    sn = jnp.sin(ang)
    c_m, c_r = cs[:ROPE_HALF], cs[ROPE_HALF:]
    s_m, s_r = sn[:ROPE_HALF], sn[ROPE_HALF:]
    ones = jnp.ones((MLA_NOPE, tm), jnp.float32)
    zeros_hi = jnp.zeros((HEAD_LANES - MLA_QK, tm), jnp.float32)
    zeros_lo = jnp.zeros((MLA_NOPE, tm), jnp.float32)
    cos_k = jnp.concatenate([ones, c_m, c_m, zeros_hi], axis=0).T
    sin_k = jnp.concatenate([zeros_lo, -s_m, s_m, zeros_hi], axis=0).T
    cos_r = jnp.concatenate([c_r] * RET_HEADS, axis=0).T
    sin_r = jnp.concatenate([s_r] * RET_HEADS, axis=0).T
    scale = MLA_QK ** -0.5 * LOG2E

    cq_kpe = _dot(h, win_ref[:, _O_CQ:_O_CKV])
    cq = cq_kpe[:, :MLA_Q_RANK]
    ckv = _dot(h, win_ref[:, _O_CKV:_O_CKV + MLA_KV_RANK])
    kpe = cq_kpe[:, MLA_Q_RANK:]
    rv_ref[rows, :] = _bf16(_dot(h, win_ref[:, _O_RV:_O_RV + RET_WIDTH]))
    rg_ref[rows, :] = _bf16(_dot(h, win_ref[:, _O_RG:_O_RG + RET_WIDTH]))
    qt = _dot_nt(wqbt_ref[...], _bf16(_rms(cq, gq_ref[...])))
    cos_q, sin_q = c_m * scale, s_m * scale
    for hd in range(MLA_HEADS):
        base = hd * HEAD_LANES
        x1 = qt[base + MLA_NOPE:base + MLA_NOPE + ROPE_HALF]
        x2 = qt[base + MLA_NOPE + ROPE_HALF:base + MLA_QK]
        head = jnp.concatenate([qt[base:base + MLA_NOPE] * scale,
                                x1 * cos_q - x2 * sin_q, x1 * sin_q + x2 * cos_q,
                                qt[base + MLA_QK:base + HEAD_LANES]], axis=0)
        qt_ref[0, base:base + HEAD_LANES, rows] = _bf16(head)

    ckvn = _bf16(_rms(ckv, gkv_ref[...]))
    kpe = kpe * cos_k + _swap_rope_halves(kpe) * sin_k
    kn = _dot(ckvn, wk_ref[...])
    for hd in range(MLA_HEADS):
        sl = slice(hd * HEAD_LANES, (hd + 1) * HEAD_LANES)
        k_ref[rows, sl] = _bf16(kn[:, sl] + kpe)
    vt = _bf16(_dot_nt(wvt_ref[...], ckvn))
    per_group = tm // TQ
    for si in range(per_group):
        vt_ref[0, g * per_group + si] = vt[:, si * TQ:(si + 1) * TQ]

    for off, o_ref in ((_O_RQ, rq_ref), (_O_RK, rk_ref)):
        z = _dot(h, win_ref[:, off:off + 2 * LANES])
        z1, z2 = z[:, :LANES], z[:, LANES:]
        o_ref[rows, :LANES] = _bf16(z1 * cos_r - z2 * sin_r)
        o_ref[rows, LANES:] = _bf16(z1 * sin_r + z2 * cos_r)


def _inproj_call(x2, mod3, pos3, inv_col, gpre, win_p, gq, wqb_p, gkv, wk_p, wv_p, seq):
    t, d = x2.shape
    tm = TM_IN
    per_b = seq // tm
    tok = lambda n: pl.BlockSpec((tm, n), lambda i: (i, 0))
    mod_spec = lambda k: pl.BlockSpec((1, 1, d), lambda i: ((i // per_b) * 6 + k, 0, 0))
    tok_widths = [MLA_HEADS * HEAD_LANES, RET_HEADS * RET_QK, RET_HEADS * RET_QK, RET_WIDTH, RET_WIDTH]
    qt_spec = pl.BlockSpec((1, MLA_HEADS * HEAD_LANES, tm), lambda i: (i // per_b, 0, i % per_b))
    qt_shape = jax.ShapeDtypeStruct((t // seq, MLA_HEADS * HEAD_LANES, seq), jnp.bfloat16)
    slabs = tm // TQ
    vt_spec = pl.BlockSpec((1, slabs, MLA_WIDTH, TQ), lambda i: (i // per_b, i % per_b, 0, 0))
    vt_shape = jax.ShapeDtypeStruct((t // seq, seq // TQ, MLA_WIDTH, TQ), jnp.bfloat16)
    tok_specs = [tok(n) for n in tok_widths]
    tok_shapes = [jax.ShapeDtypeStruct((t, n), jnp.bfloat16) for n in tok_widths]
    return pl.pallas_call(
        _inproj_kernel,
        grid=(t // tm,),
        in_specs=[tok(d), mod_spec(1), mod_spec(0),
                  pl.BlockSpec((1, 1, tm), lambda i: (i, 0, 0)),
                  _const_spec(inv_col.shape), _const_spec(gpre.shape), _const_spec(win_p.shape),
                  _const_spec(gq.shape), _const_spec(wqb_p.shape), _const_spec(gkv.shape),
                  _const_spec(wk_p.shape), _const_spec(wv_p.shape)],
        out_specs=[qt_spec, tok_specs[0], vt_spec] + tok_specs[1:],
        out_shape=[qt_shape, tok_shapes[0], vt_shape] + tok_shapes[1:],
        compiler_params=pltpu.CompilerParams(dimension_semantics=("arbitrary",),
                                             vmem_limit_bytes=VMEM_LIMIT),
        name="inproj",
    )(x2, mod3, mod3, pos3, inv_col, gpre, win_p, gq, wqb_p, gkv, wk_p, wv_p)


Q_SUB = 256
PV_LAG = 2
ONES_ROWS = 16


def _attn_kernel(qt_ref, k_ref, vt_ref, bias_ref, o_ref, m_ref, acc_ref, sa_ref, sb_ref, mxa_ref, mxb_ref):
    tk, tq = TQ, AQ
    r_diag = tq // tk
    n_q = qt_ref.shape[2] // tq
    ones = jnp.ones((ONES_ROWS, tk), jnp.bfloat16)
    subs = [(hd, qo) for hd in range(2) for qo in range(0, tq, Q_SUB)]
    full = ["full"] * len(subs)

    def diag_modes(r):
        modes = []
        for _, qo in subs:
            d = qo - r * tk
            modes.append(None if d + Q_SUB <= 0 else ("full" if d >= tk else d))
        return modes

    def key_extent(mode):
        return tk if mode == "full" else min(tk, mode + Q_SUB)

    def produce(qi, j, s_ref, mx_ref, modes, only=None):
        start = j * tk if isinstance(j, int) else pl.multiple_of(j * tk, tk)
        for i, (hd, qo) in enumerate(subs):
            if modes[i] is None or (only is not None and i != only):
                continue
            sl = slice(hd * HEAD_LANES, (hd + 1) * HEAD_LANES)
            qt = qt_ref[0, sl, qi * tq + qo:qi * tq + qo + Q_SUB]
            kx = key_extent(modes[i])
            st = _dot(k_ref[0, pl.ds(start, kx), sl], qt)
            if modes[i] != "full":
                st = st + bias_ref[:kx, modes[i]:modes[i] + Q_SUB]
            s_ref[i, :kx] = st
            mx_ref[i] = jnp.max(st, axis=0, keepdims=True)

    def step(prod, cons):
        for i in range(len(subs) + PV_LAG):
            if prod is not None and i < len(subs):
                produce(*prod, only=i)
            if cons is not None and i >= PV_LAG:
                consume(*cons, only=i - PV_LAG)

    def consume(j, s_ref, mx_ref, modes, only=None):
        for i, (hd, qo) in enumerate(subs):
            if modes[i] is None or (only is not None and i != only):
                continue
            qs = slice(qo, qo + Q_SUB)
            kx = key_extent(modes[i])
            vt = vt_ref[0, j, hd * MLA_V:(hd + 1) * MLA_V, :kx]
            lhs = jnp.concatenate([vt, ones[:, :kx]], axis=0)
            st = s_ref[i, :kx]
            m_prev = m_ref[hd, :, qs]
            m_new = jnp.maximum(m_prev, mx_ref[i])
            alpha = jnp.exp2(m_prev - m_new)
            p = _bf16(jnp.exp2(st - m_new))
            acc_ref[hd, :, qs] = alpha * acc_ref[hd, :, qs] + _dot(lhs, p)
            m_ref[hd, :, qs] = m_new

    def finalize(qi):
        outs = []
        for hd in range(2):
            acc = acc_ref[hd]
            outs.append(acc[:MLA_V] / acc[MLA_V:MLA_V + 1])
        o_ref[0, qi * tq:(qi + 1) * tq, :] = _bf16(jnp.concatenate(outs, axis=0).T)

    def tile_modes(qi, j):
        n_full = r_diag * qi
        return full if j < n_full else diag_modes(j - n_full)

    buf_a, buf_b = (sa_ref, mxa_ref), (sb_ref, mxb_ref)
    produce(0, 0, *buf_a, tile_modes(0, 0))
    for qi in range(n_q):
        m_ref[...] = jnp.full(m_ref.shape, -jnp.inf, jnp.float32)
        acc_ref[...] = jnp.zeros(acc_ref.shape, jnp.float32)
        n_full = r_diag * qi
        last = n_full + r_diag - 1
        n_pairs = max(0, (n_full - 1) // 2)

        def pair_body(i, carry, qi=qi, buf_a=buf_a, buf_b=buf_b):
            j = 2 * i
            step((qi, j + 1, *buf_b, full), (j, *buf_a, full))
            step((qi, j + 2, *buf_a, full), (j + 1, *buf_b, full))
            return carry

        if n_pairs > 0:
            lax.fori_loop(0, n_pairs, pair_body, 0)
        for j in range(2 * n_pairs, last + 1):
            if j < last:
                nxt = (qi, j + 1, *buf_b, tile_modes(qi, j + 1))
            elif qi + 1 < n_q:
                nxt = (qi + 1, 0, *buf_b, tile_modes(qi + 1, 0))
            else:
                nxt = None
            step(nxt, (j, *buf_a, tile_modes(qi, j)))
            buf_a, buf_b = buf_b, buf_a
        finalize(qi)


def _attn_call(qt, k, vt):
    bsz, seq, _ = k.shape
    tk, tq = TQ, AQ
    pairs = MLA_HEADS // 2
    n_sub = 2 * (tq // Q_SUB)
    idx = jnp.arange(tk)
    bias = jnp.where(idx[:, None] <= idx[None, :], 0.0, -jnp.inf).astype(jnp.float32)
    return pl.pallas_call(
        _attn_kernel,
        grid=(bsz, pairs),
        in_specs=[pl.BlockSpec((1, 2 * HEAD_LANES, seq), lambda b, p: (b, p, 0)),
                  pl.BlockSpec((1, seq, 2 * HEAD_LANES), lambda b, p: (b, 0, p)),
                  pl.BlockSpec((1, seq // tk, 2 * MLA_V, tk), lambda b, p: (b, 0, p, 0)),
                  _const_spec(bias.shape)],
        out_specs=pl.BlockSpec((1, seq, LANES), lambda b, p: (b, 0, p)),
        out_shape=jax.ShapeDtypeStruct((bsz, seq, MLA_WIDTH), jnp.bfloat16),
        scratch_shapes=[pltpu.VMEM((2, 1, tq), jnp.float32),
                        pltpu.VMEM((2, MLA_V + ONES_ROWS, tq), jnp.float32)]
        + [pltpu.VMEM((n_sub, tk, Q_SUB), jnp.float32)] * 2
        + [pltpu.VMEM((n_sub, 1, Q_SUB), jnp.float32)] * 2,
        compiler_params=pltpu.CompilerParams(
            dimension_semantics=("arbitrary", "arbitrary"),
            vmem_limit_bytes=VMEM_LIMIT),
        name="mla_attn",
    )(qt, k, vt, bias)


def _ret_kernel(q_ref, k_ref, v_ref, g_ref, hmask_ref, dec_ref, wq_ref, wk_ref, cd_ref, gain_ref,
                o_ref, state_ref):
    @pl.when(pl.program_id(1) == 0)
    def _():
        state_ref[...] = jnp.zeros(state_ref.shape, jnp.float32)

    c = RET_C
    for ci in range(q_ref.shape[1] // c):
        rows = slice(ci * c, (ci + 1) * c)
        q = q_ref[0, rows, :]
        k = k_ref[0, rows, :]
        for hd in range(RET_HEADS):
            vh = v_ref[0, rows, hd * RET_V:(hd + 1) * RET_V]
            qm = q * hmask_ref[hd]
            sc = _dot_nt(qm, k) * dec_ref[hd]
            inner = _dot(_bf16(sc), vh)
            st = state_ref[hd]
            cross = _dot(qm, _bf16(st)) * wq_ref[hd]
            u = _dot_tn(k, _bf16(vh.astype(jnp.float32) * wk_ref[hd]))
            state_ref[hd] = st * cd_ref[hd] + u
            o = inner + cross
            mu = jnp.mean(o, axis=-1, keepdims=True)
            var = jnp.mean(jnp.square(o - mu), axis=-1, keepdims=True)
            cols = slice(hd * RET_V, (hd + 1) * RET_V)
            on = (o - mu) * lax.rsqrt(var + EPS) * gain_ref[:, cols]
            g = g_ref[0, rows, cols].astype(jnp.float32)
            o_ref[0, rows, cols] = _bf16(g * jax.nn.sigmoid(g) * on)


def _ret_tables():
    c = RET_C
    f32 = jnp.float32
    log_gamma = jnp.log(1.0 - 2.0 ** (-5.0 - jnp.arange(RET_HEADS, dtype=f32)))
    idx = jnp.arange(c)
    rel = idx[:, None] - idx[None, :]
    k_scale = RET_QK ** -0.5
    dec = jnp.where(rel >= 0, jnp.exp(log_gamma[:, None, None] * jnp.maximum(rel, 0).astype(f32)), 0.0)
    dec = dec * k_scale
    w_q = jnp.exp(log_gamma[:, None] * (idx + 1).astype(f32))[:, :, None]
    w_k = (jnp.exp(log_gamma[:, None] * (c - 1 - idx).astype(f32)) * k_scale)[:, :, None]
    cd = jnp.exp(log_gamma * c)[:, None, None]
    w_q = jnp.broadcast_to(w_q, (RET_HEADS, c, RET_V))
    w_k = jnp.broadcast_to(w_k, (RET_HEADS, c, RET_V))
    cd = jnp.broadcast_to(cd, (RET_HEADS, 1, LANES))
    lane = jnp.arange(2 * LANES)
    hmask = ((lane % LANES) // RET_HALF)[None, :] == jnp.arange(RET_HEADS)[:, None]
    hmask = jnp.broadcast_to(hmask[:, None, :], (RET_HEADS, c, 2 * LANES)).astype(jnp.bfloat16)
    return hmask, dec, w_q, w_k, cd


def _ret_call(rq, rk, rv, rg, gn_gain):
    bsz, seq, _ = rq.shape
    tt = RET_T
    consts = _ret_tables() + (gn_gain,)
    tok = lambda n: pl.BlockSpec((1, tt, n), lambda b, i: (b, i, 0))
    return pl.pallas_call(
        _ret_kernel,
        grid=(bsz, seq // tt),
        in_specs=[tok(2 * LANES), tok(2 * LANES), tok(RET_WIDTH), tok(RET_WIDTH)]
        + [_const_spec(a.shape) for a in consts],
        out_specs=tok(RET_WIDTH),
        out_shape=jax.ShapeDtypeStruct((bsz, seq, RET_WIDTH), jnp.bfloat16),
        scratch_shapes=[pltpu.VMEM((RET_HEADS, 2 * LANES, RET_V), jnp.float32)],
        compiler_params=pltpu.CompilerParams(dimension_semantics=("arbitrary", "arbitrary"),
                                             vmem_limit_bytes=VMEM_LIMIT),
        name="retention",
    )(rq, rk, rv, rg, *consts)


def _outffn_kernel(x_ref, ym_ref, yr_ref, g1_ref, sh2_ref, sc2_ref, g2_ref,
                   gmla_ref, gpost_ref, gpre2_ref, gpost2_ref,
                   wout_ref, wg_ref, wu_ref, wd_ref, o_ref, h_ref, a_ref):
    groups = [slice(r, r + FF_ROWS) for r in range(0, x_ref.shape[0], FF_ROWS)]
    for rows in groups:
        ymn = _bf16(_rms(ym_ref[rows, :].astype(jnp.float32), gmla_ref[...]))
        mix = _dot(ymn, wout_ref[:MLA_WIDTH, :]) + _dot(yr_ref[rows, :], wout_ref[MLA_WIDTH:, :])
        x1 = x_ref[rows, :] + g1_ref[0] * _rms(mix, gpost_ref[...])
        o_ref[rows, :] = x1
        h_ref[rows, :] = _bf16(_rms(x1, gpre2_ref[...]) * (1.0 + sc2_ref[0]) + sh2_ref[0])
    for rows in groups:
        for ci in range(wg_ref.shape[1] // FF_CHUNK):
            cols = slice(ci * FF_CHUNK, (ci + 1) * FF_CHUNK)
            h = h_ref[rows, :]
            g = _dot(h, wg_ref[:, cols])
            u = _dot(h, wu_ref[:, cols])
            a_ref[rows, cols] = _bf16(g * jax.nn.sigmoid(g) * u)
    for rows in groups:
        f = _dot(a_ref[rows, :], wd_ref[...])
        o_ref[rows, :] = o_ref[rows, :] + g2_ref[0] * _rms(f, gpost2_ref[...])


def _outffn_call(x2, ym, yr, mod3, gmla, gpost, gpre2, gpost2, wout, wg3, wu3, wd, seq):
    t, d = x2.shape
    tm = TM_OUT
    per_b = seq // tm
    d_ff = wd.shape[0]
    tok = lambda n: pl.BlockSpec((tm, n), lambda i: (i, 0))
    mod_spec = lambda k: pl.BlockSpec((1, 1, d), lambda i: ((i // per_b) * 6 + k, 0, 0))
    return pl.pallas_call(
        _outffn_kernel,
        grid=(t // tm,),
        in_specs=[tok(d), tok(MLA_WIDTH), tok(RET_WIDTH),
                  mod_spec(2), mod_spec(3), mod_spec(4), mod_spec(5),
                  _const_spec(gmla.shape), _const_spec(gpost.shape), _const_spec(gpre2.shape),
                  _const_spec(gpost2.shape), _const_spec(wout.shape), _const_spec(wg3.shape),
                  _const_spec(wu3.shape), _const_spec(wd.shape)],
        out_specs=tok(d),
        out_shape=jax.ShapeDtypeStruct((t, d), jnp.float32),
        scratch_shapes=[pltpu.VMEM((tm, d), jnp.bfloat16), pltpu.VMEM((tm, d_ff), jnp.bfloat16)],
        compiler_params=pltpu.CompilerParams(dimension_semantics=("arbitrary",),
                                             vmem_limit_bytes=VMEM_LIMIT),
        name="outproj_ffn",
    )(x2, ym, yr, mod3, mod3, mod3, mod3, gmla, gpost, gpre2, gpost2, wout, wg3, wu3, wd)


def _relayout_w_in(w_in):
    d = w_in.shape[0]
    o = np.cumsum([0, MLA_Q_RANK, MLA_KV_RANK, MLA_ROPE, RET_HEADS * RET_QK, RET_HEADS * RET_QK,
                   RET_WIDTH, RET_WIDTH])
    cq, ckv, kpe, rq, rk, rv, rg = [w_in[:, o[i]:o[i + 1]] for i in range(7)]
    kpe = jnp.concatenate([jnp.zeros((d, MLA_NOPE), w_in.dtype), kpe,
                           jnp.zeros((d, HEAD_LANES - MLA_QK), w_in.dtype)], axis=1)

    def halves_first(w):
        return w.reshape(d, RET_HEADS, 2, RET_HALF).transpose(0, 2, 1, 3).reshape(d, RET_HEADS * RET_QK)

    return _bf16(jnp.concatenate([cq, kpe, ckv, halves_first(rq), halves_first(rk), rv, rg], axis=1))


def _relayout_w_q_b(w):
    r = w.shape[0]
    w = w.reshape(r, MLA_HEADS, MLA_QK)
    w = jnp.pad(w, ((0, 0), (0, 0), (0, HEAD_LANES - MLA_QK)))
    return _bf16(w.reshape(r, MLA_HEADS * HEAD_LANES).T)


def _relayout_w_kv_b(w):
    r = w.shape[0]
    w = w.reshape(r, MLA_HEADS, MLA_NOPE + MLA_V)
    wk = jnp.pad(w[..., :MLA_NOPE], ((0, 0), (0, 0), (0, HEAD_LANES - MLA_NOPE)))
    wv = w[..., MLA_NOPE:]
    return _bf16(wk.reshape(r, MLA_HEADS * HEAD_LANES)), _bf16(wv.reshape(r, MLA_WIDTH).T)


def _rope_inv_col():
    inv_m = ROPE_BASE ** (-jnp.arange(0, MLA_ROPE, 2, dtype=jnp.float32) / MLA_ROPE)
    inv_r = ROPE_BASE ** (-jnp.arange(0, RET_QK, 2, dtype=jnp.float32) / RET_QK)
    return jnp.concatenate([inv_m, inv_r]).reshape(N_FREQ, 1)


def kernel(x, c, positions, w_ada, b_ada, pre_norm_mix, w_in, q_a_norm, w_q_b, kv_a_norm, w_kv_b,
           mla_out_norm, ret_gn_gain, w_out, post_norm_mix, pre_norm_ffn, w_gate, w_up, w_down,
           post_norm_ffn):
    bsz, seq, d = x.shape
    t = bsz * seq
    depth = w_ada.shape[0]
    d_ff = w_gate.shape[-1]
    assert seq % TM_IN == 0 and seq % AQ == 0 and seq % RET_T == 0 and seq % TM_OUT == 0
    assert AQ % TQ == 0 and TQ % Q_SUB == 0
    assert RET_T % RET_C == 0 and d_ff % FF_CHUNK == 0
    assert IN_ROWS % TQ == 0 and TM_IN % IN_ROWS == 0 and TM_OUT % FF_ROWS == 0

    row = lambda g: g.reshape(1, -1)
    pos3 = positions.reshape(t // TM_IN, 1, TM_IN)
    inv_col = _rope_inv_col()
    x2 = x.reshape(t, d)
    for l in range(depth):
        mod3 = _mod_call(c, w_ada[l], b_ada[l]).reshape(bsz * 6, 1, d)
        wk_p, wv_p = _relayout_w_kv_b(w_kv_b[l])
        q, k, v, rq, rk, rv, rg = _inproj_call(
            x2, mod3, pos3, inv_col, row(pre_norm_mix[l]), _relayout_w_in(w_in[l]),
            row(q_a_norm[l]), _relayout_w_q_b(w_q_b[l]), row(kv_a_norm[l]), wk_p, wv_p, seq)
        b3 = lambda a: a.reshape(bsz, seq, a.shape[-1])
        y_mla = _attn_call(q, b3(k), v)
        y_ret = _ret_call(b3(rq), b3(rk), b3(rv), b3(rg), row(ret_gn_gain[l]))
        x2 = _outffn_call(
            x2, y_mla.reshape(t, MLA_WIDTH), y_ret.reshape(t, RET_WIDTH), mod3,
            row(mla_out_norm[l]), row(post_norm_mix[l]), row(pre_norm_ffn[l]), row(post_norm_ffn[l]),
            _bf16(w_out[l]), _bf16(w_gate[l]), _bf16(w_up[l]), _bf16(w_down[l]), seq)
    return x2.reshape(bsz, seq, d)
```

```python
import jax
import jax.numpy as jnp
import numpy as np
from jax import lax
from jax.experimental import pallas as pl
from jax.experimental.pallas import tpu as pltpu

MLA_HEADS = 8
MLA_NOPE = 64
MLA_ROPE = 32
MLA_V = 64
MLA_Q_RANK = 384
MLA_KV_RANK = 256
MLA_QK = MLA_NOPE + MLA_ROPE
RET_HEADS = 4
RET_QK = 64
RET_V = 128
MLA_WIDTH = MLA_HEADS * MLA_V
RET_WIDTH = RET_HEADS * RET_V
ROPE_BASE = 10000.0
EPS = 1e-6
LOG2E = 1.4426950408889634

LANES = 128
VMEM_LIMIT = 56 * 1024 * 1024

TM_IN = 1024
IN_ROWS = 1024
TQ = 512
AQ = 512
RET_C = 256
RET_T = 2048
TM_OUT = 1024
FF_CHUNK = 256
FF_ROWS = 512

HEAD_LANES = LANES
ROPE_HALF = MLA_ROPE // 2
RET_HALF = RET_QK // 2
N_FREQ = ROPE_HALF + RET_HALF


def _bf16(x):
    return x.astype(jnp.bfloat16)


def _dot(a, b):
    return jnp.dot(a, b, preferred_element_type=jnp.float32)


def _dot_nt(a, b):
    return lax.dot_general(a, b, (((1,), (1,)), ((), ())), preferred_element_type=jnp.float32)


def _dot_tn(a, b):
    return lax.dot_general(a, b, (((0,), (0,)), ((), ())), preferred_element_type=jnp.float32)


def _rms(x, gain):
    return x * lax.rsqrt(jnp.mean(x * x, axis=-1, keepdims=True) + EPS) * gain


def _const_spec(shape):
    nd = len(shape)
    return pl.BlockSpec(shape, lambda *_: (0,) * nd, pipeline_mode=pl.Buffered(1))


def _mod_kernel(c_ref, w_ref, b_ref, o_ref):
    c = c_ref[...]
    a = _bf16(c * jax.nn.sigmoid(c))
    o_ref[...] = _dot(a, _bf16(w_ref[...])) + b_ref[...]


def _mod_call(c, w_ada, b_ada):
    bsz, d = c.shape
    n = w_ada.shape[1]
    tn = 1024
    return pl.pallas_call(
        _mod_kernel,
        grid=(n // tn,),
        in_specs=[pl.BlockSpec((bsz, d), lambda j: (0, 0)),
                  pl.BlockSpec((d, tn), lambda j: (0, j)),
                  pl.BlockSpec((1, tn), lambda j: (0, j))],
        out_specs=pl.BlockSpec((bsz, tn), lambda j: (0, j)),
        out_shape=jax.ShapeDtypeStruct((bsz, n), jnp.float32),
        compiler_params=pltpu.CompilerParams(dimension_semantics=("arbitrary",),
                                             vmem_limit_bytes=VMEM_LIMIT),
        name="adaln_mod",
    )(c, w_ada, b_ada.reshape(1, n))


_O_CQ = 0
_O_KPE = _O_CQ + MLA_Q_RANK
_O_CKV = _O_KPE + HEAD_LANES
_O_RQ = _O_CKV + MLA_KV_RANK
_O_RK = _O_RQ + RET_HEADS * RET_QK
_O_RV = _O_RK + RET_HEADS * RET_QK
_O_RG = _O_RV + RET_WIDTH


def _swap_rope_halves(x):
    lane = lax.broadcasted_iota(jnp.int32, x.shape, 1)
    from_right = pltpu.roll(x, HEAD_LANES - ROPE_HALF, 1)
    from_left = pltpu.roll(x, ROPE_HALF, 1)
    return jnp.where(lane < MLA_NOPE + ROPE_HALF, from_right, from_left)


def _inproj_kernel(x_ref, sc_ref, sh_ref, pos_ref, inv_ref, gpre_ref, win_ref, gq_ref, wqbt_ref,
                   gkv_ref, wk_ref, wvt_ref,
                   qt_ref, k_ref, vt_ref, rq_ref, rk_ref, rv_ref, rg_ref):
    for g in range(x_ref.shape[0] // IN_ROWS):
        _inproj_rows(g, x_ref, sc_ref, sh_ref, pos_ref, inv_ref, gpre_ref, win_ref, gq_ref, wqbt_ref,
                     gkv_ref, wk_ref, wvt_ref, qt_ref, k_ref, vt_ref, rq_ref, rk_ref, rv_ref, rg_ref)


def _inproj_rows(g, x_ref, sc_ref, sh_ref, pos_ref, inv_ref, gpre_ref, win_ref, gq_ref, wqbt_ref,
                 gkv_ref, wk_ref, wvt_ref, qt_ref, k_ref, vt_ref, rq_ref, rk_ref, rv_ref, rg_ref):
    tm = IN_ROWS
    rows = slice(g * tm, (g + 1) * tm)
    x = x_ref[rows, :]
    h = _bf16(_rms(x, gpre_ref[...] * (1.0 + sc_ref[0])) + sh_ref[0])

    pos = pos_ref[0, :, rows].astype(jnp.float32)
    ang = inv_ref[...] * pos
    cs = jnp.cos(ang)
    sn = jnp.sin(ang)
    c_m, c_r = cs[:ROPE_HALF], cs[ROPE_HALF:]
    s_m, s_r = sn[:ROPE_HALF], sn[ROPE_HALF:]
    ones = jnp.ones((MLA_NOPE, tm), jnp.float32)
    zeros_hi = jnp.zeros((HEAD_LANES - MLA_QK, tm), jnp.float32)
    zeros_lo = jnp.zeros((MLA_NOPE, tm), jnp.float32)
    cos_k = jnp.concatenate([ones, c_m, c_m, zeros_hi], axis=0).T
    sin_k = jnp.concatenate([zeros_lo, -s_m, s_m, zeros_hi], axis=0).T
    cos_r = jnp.concatenate([c_r] * RET_HEADS, axis=0).T
    sin_r = jnp.concatenate([s_r] * RET_HEADS, axis=0).T
    scale = MLA_QK ** -0.5 * LOG2E

    cq_kpe = _dot(h, win_ref[:, _O_CQ:_O_CKV])
    cq = cq_kpe[:, :MLA_Q_RANK]
    ckv = _dot(h, win_ref[:, _O_CKV:_O_CKV + MLA_KV_RANK])
    kpe = cq_kpe[:, MLA_Q_RANK:]
    rv_ref[rows, :] = _bf16(_dot(h, win_ref[:, _O_RV:_O_RV + RET_WIDTH]))
    rg_ref[rows, :] = _bf16(_dot(h, win_ref[:, _O_RG:_O_RG + RET_WIDTH]))
    qt = _dot_nt(wqbt_ref[...], _bf16(_rms(cq, gq_ref[...])))
    cos_q, sin_q = c_m * scale, s_m * scale
    for hd in range(MLA_HEADS):
        base = hd * HEAD_LANES
        x1 = qt[base + MLA_NOPE:base + MLA_NOPE + ROPE_HALF]
        x2 = qt[base + MLA_NOPE + ROPE_HALF:base + MLA_QK]
        head = jnp.concatenate([qt[base:base + MLA_NOPE] * scale,
                                x1 * cos_q - x2 * sin_q, x1 * sin_q + x2 * cos_q,
                                qt[base + MLA_QK:base + HEAD_LANES]], axis=0)
        qt_ref[0, base:base + HEAD_LANES, rows] = _bf16(head)

    ckvn = _bf16(_rms(ckv, gkv_ref[...]))
    kpe = kpe * cos_k + _swap_rope_halves(kpe) * sin_k
    kn = _dot(ckvn, wk_ref[...])
    for hd in range(MLA_HEADS):
        sl = slice(hd * HEAD_LANES, (hd + 1) * HEAD_LANES)
        k_ref[rows, sl] = _bf16(kn[:, sl] + kpe)
    vt = _bf16(_dot_nt(wvt_ref[...], ckvn))
    per_group = tm // TQ
    for si in range(per_group):
        vt_ref[0, g * per_group + si] = vt[:, si * TQ:(si + 1) * TQ]

    for off, o_ref in ((_O_RQ, rq_ref), (_O_RK, rk_ref)):
        z = _dot(h, win_ref[:, off:off + 2 * LANES])
        z1, z2 = z[:, :LANES], z[:, LANES:]
        o_ref[rows, :LANES] = _bf16(z1 * cos_r - z2 * sin_r)
        o_ref[rows, LANES:] = _bf16(z1 * sin_r + z2 * cos_r)


def _inproj_call(x2, mod3, pos3, inv_col, gpre, win_p, gq, wqb_p, gkv, wk_p, wv_p, seq):
    t, d = x2.shape
    tm = TM_IN
    per_b = seq // tm
    tok = lambda n: pl.BlockSpec((tm, n), lambda i: (i, 0))
    mod_spec = lambda k: pl.BlockSpec((1, 1, d), lambda i: ((i // per_b) * 6 + k, 0, 0))
    tok_widths = [MLA_HEADS * HEAD_LANES, RET_HEADS * RET_QK, RET_HEADS * RET_QK, RET_WIDTH, RET_WIDTH]
    qt_spec = pl.BlockSpec((1, MLA_HEADS * HEAD_LANES, tm), lambda i: (i // per_b, 0, i % per_b))
    qt_shape = jax.ShapeDtypeStruct((t // seq, MLA_HEADS * HEAD_LANES, seq), jnp.bfloat16)
    slabs = tm // TQ
    vt_spec = pl.BlockSpec((1, slabs, MLA_WIDTH, TQ), lambda i: (i // per_b, i % per_b, 0, 0))
    vt_shape = jax.ShapeDtypeStruct((t // seq, seq // TQ, MLA_WIDTH, TQ), jnp.bfloat16)
    tok_specs = [tok(n) for n in tok_widths]
    tok_shapes = [jax.ShapeDtypeStruct((t, n), jnp.bfloat16) for n in tok_widths]
    return pl.pallas_call(
        _inproj_kernel,
        grid=(t // tm,),
        in_specs=[tok(d), mod_spec(1), mod_spec(0),
                  pl.BlockSpec((1, 1, tm), lambda i: (i, 0, 0)),
                  _const_spec(inv_col.shape), _const_spec(gpre.shape), _const_spec(win_p.shape),
                  _const_spec(gq.shape), _const_spec(wqb_p.shape), _const_spec(gkv.shape),
                  _const_spec(wk_p.shape), _const_spec(wv_p.shape)],
        out_specs=[qt_spec, tok_specs[0], vt_spec] + tok_specs[1:],
        out_shape=[qt_shape, tok_shapes[0], vt_shape] + tok_shapes[1:],
        compiler_params=pltpu.CompilerParams(dimension_semantics=("arbitrary",),
                                             vmem_limit_bytes=VMEM_LIMIT),
        name="inproj",
    )(x2, mod3, mod3, pos3, inv_col, gpre, win_p, gq, wqb_p, gkv, wk_p, wv_p)


Q_SUB = 256
PV_LAG = 2
ONES_ROWS = 16


def _attn_kernel(qt_ref, k_ref, vt_ref, bias_ref, o_ref, m_ref, acc_ref, sa_ref, sb_ref, mxa_ref, mxb_ref):
    tk, tq = TQ, AQ
    r_diag = tq // tk
    n_q = qt_ref.shape[2] // tq
    ones = jnp.ones((ONES_ROWS, tk), jnp.bfloat16)
    subs = [(hd, qo) for hd in range(2) for qo in range(0, tq, Q_SUB)]
    full = ["full"] * len(subs)

    def diag_modes(r):
        modes = []
        for _, qo in subs:
            d = qo - r * tk
            modes.append(None if d + Q_SUB <= 0 else ("full" if d >= tk else d))
        return modes

    def key_extent(mode):
        return tk if mode == "full" else min(tk, mode + Q_SUB)

    def produce(qi, j, s_ref, mx_ref, modes, only=None):
        start = j * tk if isinstance(j, int) else pl.multiple_of(j * tk, tk)
        for i, (hd, qo) in enumerate(subs):
            if modes[i] is None or (only is not None and i != only):
                continue
            sl = slice(hd * HEAD_LANES, (hd + 1) * HEAD_LANES)
            qt = qt_ref[0, sl, qi * tq + qo:qi * tq + qo + Q_SUB]
            kx = key_extent(modes[i])
            st = _dot(k_ref[0, pl.ds(start, kx), sl], qt)
            if modes[i] != "full":
                st = st + bias_ref[:kx, modes[i]:modes[i] + Q_SUB]
            s_ref[i, :kx] = st
            mx_ref[i] = jnp.max(st, axis=0, keepdims=True)

    def step(prod, cons):
        for i in range(len(subs) + PV_LAG):
            if prod is not None and i < len(subs):
                produce(*prod, only=i)
            if cons is not None and i >= PV_LAG:
                consume(*cons, only=i - PV_LAG)

    def consume(j, s_ref, mx_ref, modes, only=None):
        for i, (hd, qo) in enumerate(subs):
            if modes[i] is None or (only is not None and i != only):
                continue
            qs = slice(qo, qo + Q_SUB)
            kx = key_extent(modes[i])
            vt = vt_ref[0, j, hd * MLA_V:(hd + 1) * MLA_V, :kx]
            lhs = jnp.concatenate([vt, ones[:, :kx]], axis=0)
            st = s_ref[i, :kx]
            m_prev = m_ref[hd, :, qs]
            m_new = jnp.maximum(m_prev, mx_ref[i])
            alpha = jnp.exp2(m_prev - m_new)
            p = _bf16(jnp.exp2(st - m_new))
            acc_ref[hd, :, qs] = alpha * acc_ref[hd, :, qs] + _dot(lhs, p)
            m_ref[hd, :, qs] = m_new

    def finalize(qi):
        outs = []
        for hd in range(2):
            acc = acc_ref[hd]
            outs.append(acc[:MLA_V] / acc[MLA_V:MLA_V + 1])
        o_ref[0, qi * tq:(qi + 1) * tq, :] = _bf16(jnp.concatenate(outs, axis=0).T)

    def tile_modes(qi, j):
        n_full = r_diag * qi
        return full if j < n_full else diag_modes(j - n_full)

    buf_a, buf_b = (sa_ref, mxa_ref), (sb_ref, mxb_ref)
    produce(0, 0, *buf_a, tile_modes(0, 0))
    for qi in range(n_q):
        m_ref[...] = jnp.full(m_ref.shape, -jnp.inf, jnp.float32)
        acc_ref[...] = jnp.zeros(acc_ref.shape, jnp.float32)
        n_full = r_diag * qi
        last = n_full + r_diag - 1
        n_pairs = max(0, (n_full - 1) // 2)

        def pair_body(i, carry, qi=qi, buf_a=buf_a, buf_b=buf_b):
            j = 2 * i
            step((qi, j + 1, *buf_b, full), (j, *buf_a, full))
            step((qi, j + 2, *buf_a, full), (j + 1, *buf_b, full))
            return carry

        if n_pairs > 0:
            lax.fori_loop(0, n_pairs, pair_body, 0)
        for j in range(2 * n_pairs, last + 1):
            if j < last:
                nxt = (qi, j + 1, *buf_b, tile_modes(qi, j + 1))
            elif qi + 1 < n_q:
                nxt = (qi + 1, 0, *buf_b, tile_modes(qi + 1, 0))
            else:
                nxt = None
            step(nxt, (j, *buf_a, tile_modes(qi, j)))
            buf_a, buf_b = buf_b, buf_a
        finalize(qi)


def _attn_call(qt, k, vt):
    bsz, seq, _ = k.shape
    tk, tq = TQ, AQ
    pairs = MLA_HEADS // 2
    n_sub = 2 * (tq // Q_SUB)
    idx = jnp.arange(tk)
    bias = jnp.where(idx[:, None] <= idx[None, :], 0.0, -jnp.inf).astype(jnp.float32)
    return pl.pallas_call(
        _attn_kernel,
        grid=(bsz, pairs),
        in_specs=[pl.BlockSpec((1, 2 * HEAD_LANES, seq), lambda b, p: (b, p, 0)),
                  pl.BlockSpec((1, seq, 2 * HEAD_LANES), lambda b, p: (b, 0, p)),
                  pl.BlockSpec((1, seq // tk, 2 * MLA_V, tk), lambda b, p: (b, 0, p, 0)),
                  _const_spec(bias.shape)],
        out_specs=pl.BlockSpec((1, seq, LANES), lambda b, p: (b, 0, p)),
        out_shape=jax.ShapeDtypeStruct((bsz, seq, MLA_WIDTH), jnp.bfloat16),
        scratch_shapes=[pltpu.VMEM((2, 1, tq), jnp.float32),
                        pltpu.VMEM((2, MLA_V + ONES_ROWS, tq), jnp.float32)]
        + [pltpu.VMEM((n_sub, tk, Q_SUB), jnp.float32)] * 2
        + [pltpu.VMEM((n_sub, 1, Q_SUB), jnp.float32)] * 2,
        compiler_params=pltpu.CompilerParams(
            dimension_semantics=("arbitrary", "arbitrary"),
            vmem_limit_bytes=VMEM_LIMIT),
        name="mla_attn",
    )(qt, k, vt, bias)


def _ret_kernel(q_ref, k_ref, v_ref, g_ref, hmask_ref, dec_ref, wq_ref, wk_ref, cd_ref, gain_ref,
                o_ref, state_ref):
    @pl.when(pl.program_id(1) == 0)
    def _():
        state_ref[...] = jnp.zeros(state_ref.shape, jnp.float32)

    c = RET_C
    for ci in range(q_ref.shape[1] // c):
        rows = slice(ci * c, (ci + 1) * c)
        q = q_ref[0, rows, :]
        k = k_ref[0, rows, :]
        for hd in range(RET_HEADS):
            vh = v_ref[0, rows, hd * RET_V:(hd + 1) * RET_V]
            qm = q * hmask_ref[hd]
            sc = _dot_nt(qm, k) * dec_ref[hd]
            inner = _dot(_bf16(sc), vh)
            st = state_ref[hd]
            cross = _dot(qm, _bf16(st)) * wq_ref[hd]
            u = _dot_tn(k, _bf16(vh.astype(jnp.float32) * wk_ref[hd]))
            state_ref[hd] = st * cd_ref[hd] + u
            o = inner + cross
            mu = jnp.mean(o, axis=-1, keepdims=True)
            var = jnp.mean(jnp.square(o - mu), axis=-1, keepdims=True)
            cols = slice(hd * RET_V, (hd + 1) * RET_V)
            on = (o - mu) * lax.rsqrt(var + EPS) * gain_ref[:, cols]
            g = g_ref[0, rows, cols].astype(jnp.float32)
            o_ref[0, rows, cols] = _bf16(g * jax.nn.sigmoid(g) * on)


def _ret_tables():
    c = RET_C
    f32 = jnp.float32
    log_gamma = jnp.log(1.0 - 2.0 ** (-5.0 - jnp.arange(RET_HEADS, dtype=f32)))
    idx = jnp.arange(c)
    rel = idx[:, None] - idx[None, :]
    k_scale = RET_QK ** -0.5
    dec = jnp.where(rel >= 0, jnp.exp(log_gamma[:, None, None] * jnp.maximum(rel, 0).astype(f32)), 0.0)
    dec = dec * k_scale
    w_q = jnp.exp(log_gamma[:, None] * (idx + 1).astype(f32))[:, :, None]
    w_k = (jnp.exp(log_gamma[:, None] * (c - 1 - idx).astype(f32)) * k_scale)[:, :, None]
    cd = jnp.exp(log_gamma * c)[:, None, None]
    w_q = jnp.broadcast_to(w_q, (RET_HEADS, c, RET_V))
    w_k = jnp.broadcast_to(w_k, (RET_HEADS, c, RET_V))
    cd = jnp.broadcast_to(cd, (RET_HEADS, 1, LANES))
    lane = jnp.arange(2 * LANES)
    hmask = ((lane % LANES) // RET_HALF)[None, :] == jnp.arange(RET_HEADS)[:, None]
    hmask = jnp.broadcast_to(hmask[:, None, :], (RET_HEADS, c, 2 * LANES)).astype(jnp.bfloat16)
    return hmask, dec, w_q, w_k, cd


def _ret_call(rq, rk, rv, rg, gn_gain):
    bsz, seq, _ = rq.shape
    tt = RET_T
    consts = _ret_tables() + (gn_gain,)
    tok = lambda n: pl.BlockSpec((1, tt, n), lambda b, i: (b, i, 0))
    return pl.pallas_call(
        _ret_kernel,
        grid=(bsz, seq // tt),
        in_specs=[tok(2 * LANES), tok(2 * LANES), tok(RET_WIDTH), tok(RET_WIDTH)]
        + [_const_spec(a.shape) for a in consts],
        out_specs=tok(RET_WIDTH),
        out_shape=jax.ShapeDtypeStruct((bsz, seq, RET_WIDTH), jnp.bfloat16),
        scratch_shapes=[pltpu.VMEM((RET_HEADS, 2 * LANES, RET_V), jnp.float32)],
        compiler_params=pltpu.CompilerParams(dimension_semantics=("arbitrary", "arbitrary"),
                                             vmem_limit_bytes=VMEM_LIMIT),
        name="retention",
    )(rq, rk, rv, rg, *consts)


def _outffn_kernel(x_ref, ym_ref, yr_ref, g1_ref, sh2_ref, sc2_ref, g2_ref,
                   gmla_ref, gpost_ref, gpre2_ref, gpost2_ref,
                   wout_ref, wg_ref, wu_ref, wd_ref, o_ref, h_ref, a_ref):
    groups = [slice(r, r + FF_ROWS) for r in range(0, x_ref.shape[0], FF_ROWS)]
    for rows in groups:
        ymn = _bf16(_rms(ym_ref[rows, :].astype(jnp.float32), gmla_ref[...]))
        mix = _dot(ymn, wout_ref[:MLA_WIDTH, :]) + _dot(yr_ref[rows, :], wout_ref[MLA_WIDTH:, :])
        x1 = x_ref[rows, :] + _rms(mix, gpost_ref[...] * g1_ref[0])
        o_ref[rows, :] = x1
        h_ref[rows, :] = _bf16(_rms(x1, gpre2_ref[...] * (1.0 + sc2_ref[0])) + sh2_ref[0])
    for rows in groups:
        for ci in range(wg_ref.shape[1] // FF_CHUNK):
            cols = slice(ci * FF_CHUNK, (ci + 1) * FF_CHUNK)
            h = h_ref[rows, :]
            g = _dot(h, wg_ref[:, cols])
            u = _dot(h, wu_ref[:, cols])
            a_ref[rows, cols] = _bf16(g * jax.nn.sigmoid(g) * u)
    for rows in groups:
        f = _dot(a_ref[rows, :], wd_ref[...])
        o_ref[rows, :] = o_ref[rows, :] + _rms(f, gpost2_ref[...] * g2_ref[0])


def _outffn_call(x2, ym, yr, mod3, gmla, gpost, gpre2, gpost2, wout, wg3, wu3, wd, seq):
    t, d = x2.shape
    tm = TM_OUT
    per_b = seq // tm
    d_ff = wd.shape[0]
    tok = lambda n: pl.BlockSpec((tm, n), lambda i: (i, 0))
    mod_spec = lambda k: pl.BlockSpec((1, 1, d), lambda i: ((i // per_b) * 6 + k, 0, 0))
    return pl.pallas_call(
        _outffn_kernel,
        grid=(t // tm,),
        in_specs=[tok(d), tok(MLA_WIDTH), tok(RET_WIDTH),
                  mod_spec(2), mod_spec(3), mod_spec(4), mod_spec(5),
                  _const_spec(gmla.shape), _const_spec(gpost.shape), _const_spec(gpre2.shape),
                  _const_spec(gpost2.shape), _const_spec(wout.shape), _const_spec(wg3.shape),
                  _const_spec(wu3.shape), _const_spec(wd.shape)],
        out_specs=tok(d),
        out_shape=jax.ShapeDtypeStruct((t, d), jnp.float32),
        scratch_shapes=[pltpu.VMEM((tm, d), jnp.bfloat16), pltpu.VMEM((tm, d_ff), jnp.bfloat16)],
        compiler_params=pltpu.CompilerParams(dimension_semantics=("arbitrary",),
                                             vmem_limit_bytes=VMEM_LIMIT),
        name="outproj_ffn",
    )(x2, ym, yr, mod3, mod3, mod3, mod3, gmla, gpost, gpre2, gpost2, wout, wg3, wu3, wd)


def _relayout_w_in(w_in):
    d = w_in.shape[0]
    o = np.cumsum([0, MLA_Q_RANK, MLA_KV_RANK, MLA_ROPE, RET_HEADS * RET_QK, RET_HEADS * RET_QK,
                   RET_WIDTH, RET_WIDTH])
    cq, ckv, kpe, rq, rk, rv, rg = [w_in[:, o[i]:o[i + 1]] for i in range(7)]
    kpe = jnp.concatenate([jnp.zeros((d, MLA_NOPE), w_in.dtype), kpe,
                           jnp.zeros((d, HEAD_LANES - MLA_QK), w_in.dtype)], axis=1)

    def halves_first(w):
        return w.reshape(d, RET_HEADS, 2, RET_HALF).transpose(0, 2, 1, 3).reshape(d, RET_HEADS * RET_QK)

    return _bf16(jnp.concatenate([cq, kpe, ckv, halves_first(rq), halves_first(rk), rv, rg], axis=1))


def _relayout_w_q_b(w):
    r = w.shape[0]
    w = w.reshape(r, MLA_HEADS, MLA_QK)
    w = jnp.pad(w, ((0, 0), (0, 0), (0, HEAD_LANES - MLA_QK)))
    return _bf16(w.reshape(r, MLA_HEADS * HEAD_LANES).T)


def _relayout_w_kv_b(w):
    r = w.shape[0]
    w = w.reshape(r, MLA_HEADS, MLA_NOPE + MLA_V)
    wk = jnp.pad(w[..., :MLA_NOPE], ((0, 0), (0, 0), (0, HEAD_LANES - MLA_NOPE)))
    wv = w[..., MLA_NOPE:]
    return _bf16(wk.reshape(r, MLA_HEADS * HEAD_LANES)), _bf16(wv.reshape(r, MLA_WIDTH).T)


def _rope_inv_col():
    inv_m = ROPE_BASE ** (-jnp.arange(0, MLA_ROPE, 2, dtype=jnp.float32) / MLA_ROPE)
    inv_r = ROPE_BASE ** (-jnp.arange(0, RET_QK, 2, dtype=jnp.float32) / RET_QK)
    return jnp.concatenate([inv_m, inv_r]).reshape(N_FREQ, 1)


def kernel(x, c, positions, w_ada, b_ada, pre_norm_mix, w_in, q_a_norm, w_q_b, kv_a_norm, w_kv_b,
           mla_out_norm, ret_gn_gain, w_out, post_norm_mix, pre_norm_ffn, w_gate, w_up, w_down,
           post_norm_ffn):
    bsz, seq, d = x.shape
    t = bsz * seq
    depth = w_ada.shape[0]
    d_ff = w_gate.shape[-1]
    assert seq % TM_IN == 0 and seq % AQ == 0 and seq % RET_T == 0 and seq % TM_OUT == 0
    assert AQ % TQ == 0 and TQ % Q_SUB == 0
    assert RET_T % RET_C == 0 and d_ff % FF_CHUNK == 0
    assert IN_ROWS % TQ == 0 and TM_IN % IN_ROWS == 0 and TM_OUT % FF_ROWS == 0

    row = lambda g: g.reshape(1, -1)
    pos3 = positions.reshape(t // TM_IN, 1, TM_IN)
    inv_col = _rope_inv_col()
    x2 = x.reshape(t, d)
    for l in range(depth):
        mod3 = _mod_call(c, w_ada[l], b_ada[l]).reshape(bsz * 6, 1, d)
        wk_p, wv_p = _relayout_w_kv_b(w_kv_b[l])
        q, k, v, rq, rk, rv, rg = _inproj_call(
            x2, mod3, pos3, inv_col, row(pre_norm_mix[l]), _relayout_w_in(w_in[l]),
            row(q_a_norm[l]), _relayout_w_q_b(w_q_b[l]), row(kv_a_norm[l]), wk_p, wv_p, seq)
        b3 = lambda a: a.reshape(bsz, seq, a.shape[-1])
        y_mla = _attn_call(q, b3(k), v)
        y_ret = _ret_call(b3(rq), b3(rk), b3(rv), b3(rg), row(ret_gn_gain[l]))
        x2 = _outffn_call(
            x2, y_mla.reshape(t, MLA_WIDTH), y_ret.reshape(t, RET_WIDTH), mod3,
            row(mla_out_norm[l]), row(post_norm_mix[l]), row(pre_norm_ffn[l]), row(post_norm_ffn[l]),
            _bf16(w_out[l]), _bf16(w_gate[l]), _bf16(w_up[l]), _bf16(w_down[l]), seq)
    return x2.reshape(bsz, seq, d)
```
